```python
import functools
import jax, jax.numpy as jnp
from jax import lax
import numpy as np

D_MODEL = 4096
BATCH = 1
SEQ = 8192
DEPTH = 1
DEC_BATCH = 32
DEC_SEQ = 4
PAST_LEN = 8192
PAGE_SIZE = 128

D_PLE = 256
CHUNK = 128
A_GROUPS = 16
A_GROUP_DIM = D_MODEL // 2 // A_GROUPS
D_A = A_GROUPS * A_GROUP_DIM
N_HEADS = 16
HEAD_DIM = D_MODEL // 2 // N_HEADS
D_B = N_HEADS * HEAD_DIM
N_IDX_HEADS = 32
IDX_DIM = 128
TOPK_MAX = 256
ROPE_THETA = 10000.0
EPS = 1e-6
Q_BLOCK = 128

IN_WIDTHS = [D_A, D_A, D_A, D_B, D_B, D_B, D_B, N_IDX_HEADS * IDX_DIM, IDX_DIM, N_IDX_HEADS, D_MODEL, D_MODEL]
D_IN_TOTAL = sum(IN_WIDTHS)
SPLIT_POINTS = [int(v) for v in np.cumsum(IN_WIDTHS)[:-1]]

kernel_name = "hybrid_gmlp_dsa_parallel_step"


def rmsnorm(x, g):
    xf = x.astype(jnp.float32)
    y = xf * lax.rsqrt(jnp.mean(xf * xf, axis=-1, keepdims=True) + EPS)
    return (y * g.astype(jnp.float32)).astype(x.dtype)


def layernorm(x, g, b):
    xf = x.astype(jnp.float32)
    mu = jnp.mean(xf, axis=-1, keepdims=True)
    xc = xf - mu
    y = xc * lax.rsqrt(jnp.mean(xc * xc, axis=-1, keepdims=True) + EPS)
    return (y * g.astype(jnp.float32) + b.astype(jnp.float32)).astype(x.dtype)


def rope(x, pos):
    half = x.shape[-1] // 2
    inv = ROPE_THETA ** (-jnp.arange(half, dtype=jnp.float32) / half)
    ang = pos.astype(jnp.float32)[:, None] * inv[None, :]
    cos = jnp.cos(ang)[None, :, None, :]
    sin = jnp.sin(ang)[None, :, None, :]
    xf = x.astype(jnp.float32)
    x1, x2 = xf[..., :half], xf[..., half:]
    return jnp.concatenate([x1 * cos - x2 * sin, x2 * cos + x1 * sin], axis=-1).astype(x.dtype)


def spatial_mix(vc, w_s, b_s):
    lc = vc.shape[2]
    mask = jnp.tril(jnp.ones((lc, lc), dtype=bool))
    ws = jnp.where(mask[None], w_s[:, :lc, :lc], 0.0).astype(vc.dtype)
    out = jnp.einsum('gts,bnsgd->bntgd', ws, vc)
    return out + b_s[:, :lc].T.astype(vc.dtype)[None, None, :, :, None]


def indexer_scores(qi, wi, ki_all, t):
    dots = jnp.einsum('bthd,bsd->bths', qi.astype(jnp.float32), ki_all.astype(jnp.float32))
    score = jnp.einsum('bth,bths->bts', wi.astype(jnp.float32), jax.nn.relu(dots))
    causal = jnp.arange(ki_all.shape[1])[None, :] <= t[:, None]
    return jnp.where(causal[None], score, -jnp.inf)


def attend(q, k_sel, v_sel, valid):
    logits = jnp.einsum('bqhd,bqkhd->bqhk', q, k_sel).astype(jnp.float32) * (HEAD_DIM ** -0.5)
    logits = jnp.where(valid[:, :, None, :], logits, -jnp.inf)
    p = jax.nn.softmax(logits, axis=-1)
    return jnp.einsum('bqhk,bqkhd->bqhd', p.astype(v_sel.dtype), v_sel)


def gather_rows(rows, idx):
    return jax.vmap(lambda r, i: r[i])(rows, idx)


def sparse_attn_prompt(q, k, v, qi, ki, wi):
    B, S = q.shape[0], q.shape[1]
    topk = min(TOPK_MAX, S // 4)
    nb = S // Q_BLOCK

    def blockify(a):
        return jnp.moveaxis(a.reshape((B, nb, Q_BLOCK) + a.shape[2:]), 1, 0)

    def one_block(args):
        i, q_blk, qi_blk, wi_blk = args
        t = i * Q_BLOCK + jnp.arange(Q_BLOCK)
        score = indexer_scores(qi_blk, wi_blk, ki, t)
        _, idx = lax.top_k(score, topk)
        valid = idx <= t[None, :, None]
        return attend(q_blk, gather_rows(k, idx), gather_rows(v, idx), valid)

    out = lax.map(one_block, (jnp.arange(nb), blockify(q), blockify(qi), blockify(wi)))
    return jnp.moveaxis(out, 0, 1).reshape(q.shape)


def sparse_attn_sample(q, k, v, qi, ki, wi, cache_k, cache_v, cache_kidx, page_table):
    Bd, Sd = q.shape[0], q.shape[1]
    past = page_table.shape[1] * PAGE_SIZE
    topk = min(TOPK_MAX, (past + Sd) // 4)
    t = past + jnp.arange(Sd)
    ki_past = cache_kidx[page_table].reshape(Bd, past, IDX_DIM)
    ki_all = jnp.concatenate([ki_past, ki.astype(ki_past.dtype)], axis=1)
    score = indexer_scores(qi, wi, ki_all, t)
    _, idx = lax.top_k(score, topk)
    valid = idx <= t[None, :, None]
    past_idx = jnp.minimum(idx, past - 1)
    phys = jax.vmap(lambda pt, ii: pt[ii // PAGE_SIZE])(page_table, past_idx)
    off = past_idx % PAGE_SIZE
    new_idx = jnp.clip(idx - past, 0, Sd - 1)
    is_new = (idx >= past)[..., None, None]
    k_sel = jnp.where(is_new, gather_rows(k, new_idx), cache_k[phys, off].astype(k.dtype))
    v_sel = jnp.where(is_new, gather_rows(v, new_idx), cache_v[phys, off].astype(v.dtype))
    return attend(q, k_sel, v_sel, valid)


def mixer_layer(x, p, pos, attn_fn, g_norm, w_in, g_va, b_va, w_s, b_s, g_q, g_k,
                w_up_a, w_up_b, w_o, g_ple, w_ple_gate, w_ple_proj):
    B, S, _ = x.shape
    h = rmsnorm(x, g_norm)
    proj = jnp.einsum('bsd,de->bse', h, w_in)
    u, va, za, q, k, vb, zb, qi, ki, wi, ga, gb = jnp.split(proj, SPLIT_POINTS, axis=-1)
    vn = layernorm(va, g_va, b_va)
    lc = min(S, CHUNK)
    vc = vn.reshape(B, S // lc, lc, A_GROUPS, A_GROUP_DIM)
    mixed = spatial_mix(vc, w_s, b_s).reshape(B, S, D_A)
    a_up = jnp.einsum('bsc,cd->bsd', u * mixed * jax.nn.silu(za), w_up_a)
    q = rope(rmsnorm(q.reshape(B, S, N_HEADS, HEAD_DIM), g_q), pos)
    k = rope(rmsnorm(k.reshape(B, S, N_HEADS, HEAD_DIM), g_k), pos)
    vb = vb.reshape(B, S, N_HEADS, HEAD_DIM)
    qi = rope(qi.reshape(B, S, N_IDX_HEADS, IDX_DIM), pos)
    ki = rope(ki[:, :, None, :], pos)[:, :, 0, :]
    wi = wi * (N_IDX_HEADS ** -0.5 * IDX_DIM ** -0.5)
    attn = attn_fn(q, k, vb, qi, ki, wi)
    b_up = jnp.einsum('bsc,cd->bsd', attn.reshape(B, S, D_B) * jax.nn.silu(zb), w_up_b)
    merged = jax.nn.sigmoid(ga) * a_up + jax.nn.sigmoid(gb) * b_up
    x1 = x + jnp.einsum('bsd,de->bse', merged, w_o)
    ple_gate = jax.nn.sigmoid(jnp.einsum('bsd,de->bse', rmsnorm(x1, g_ple), w_ple_gate))
    y = x1 + ple_gate * jnp.einsum('bsp,pd->bsd', p, w_ple_proj)
    return y, k, vb, ki, vn


def setup_inputs(seed: int = 0) -> dict:
    key = jax.random.key(seed)
    ks = jax.random.split(key, 32)
    f32 = jnp.float32

    def nrm(k, shape, scale):
        return jax.random.normal(k, shape, f32) * scale

    n_pages = PAST_LEN // PAGE_SIZE
    n_used = DEC_BATCH * n_pages
    n_phys = n_used + (n_used + 3) // 4
    page_table = jax.random.permutation(ks[0], n_phys)[:n_used].reshape(DEC_BATCH, n_pages).astype(jnp.int32)
    return {
        "x_prompt": nrm(ks[1], (BATCH, SEQ, D_MODEL), 1.0),
        "x_sample": nrm(ks[2], (DEC_BATCH, DEC_SEQ, D_MODEL), 1.0),
        "p_prompt": nrm(ks[3], (DEPTH, BATCH, SEQ, D_PLE), 1.0),
        "p_sample": nrm(ks[4], (DEPTH, DEC_BATCH, DEC_SEQ, D_PLE), 1.0),
        "cache_k": nrm(ks[5], (DEPTH, n_phys, PAGE_SIZE, N_HEADS, HEAD_DIM), 1.0),
        "cache_v": nrm(ks[6], (DEPTH, n_phys, PAGE_SIZE, N_HEADS, HEAD_DIM), 1.0),
        "cache_kidx": nrm(ks[7], (DEPTH, n_phys, PAGE_SIZE, IDX_DIM), 1.0),
        "page_table": page_table,
        "g_norm": 1.0 + nrm(ks[8], (DEPTH, D_MODEL), 0.02),
        "w_in": nrm(ks[9], (DEPTH, D_MODEL, D_IN_TOTAL), D_MODEL ** -0.5),
        "g_va": 1.0 + nrm(ks[10], (DEPTH, D_A), 0.02),
        "b_va": nrm(ks[11], (DEPTH, D_A), 0.02),
        "w_s": nrm(ks[12], (DEPTH, A_GROUPS, CHUNK, CHUNK), CHUNK ** -0.5),
        "b_s": 1.0 + nrm(ks[13], (DEPTH, A_GROUPS, CHUNK), 0.02),
        "g_q": 1.0 + nrm(ks[14], (DEPTH, HEAD_DIM), 0.02),
        "g_k": 1.0 + nrm(ks[15], (DEPTH, HEAD_DIM), 0.02),
        "w_up_a": nrm(ks[16], (DEPTH, D_A, D_MODEL), D_A ** -0.5),
        "w_up_b": nrm(ks[17], (DEPTH, D_B, D_MODEL), D_B ** -0.5),
        "w_o": nrm(ks[18], (DEPTH, D_MODEL, D_MODEL), D_MODEL ** -0.5),
        "g_ple": 1.0 + nrm(ks[19], (DEPTH, D_MODEL), 0.02),
        "w_ple_gate": nrm(ks[20], (DEPTH, D_MODEL, D_MODEL), D_MODEL ** -0.5),
        "w_ple_proj": nrm(ks[21], (DEPTH, D_PLE, D_MODEL), D_PLE ** -0.5),
    }


def reference(x_prompt, x_sample, p_prompt, p_sample, cache_k, cache_v, cache_kidx, page_table,
              g_norm, w_in, g_va, b_va, w_s, b_s, g_q, g_k, w_up_a, w_up_b, w_o,
              g_ple, w_ple_gate, w_ple_proj):
    pos_p = jnp.arange(x_prompt.shape[1])
    pos_s = page_table.shape[1] * PAGE_SIZE + jnp.arange(x_sample.shape[1])
    yp, ys = x_prompt, x_sample
    kp_l, vp_l, kip_l, ks_l, vs_l, kis_l, vas_l = [], [], [], [], [], [], []
    for i in range(DEPTH):
        w = (g_norm[i], w_in[i], g_va[i], b_va[i], w_s[i], b_s[i], g_q[i], g_k[i],
             w_up_a[i], w_up_b[i], w_o[i], g_ple[i], w_ple_gate[i], w_ple_proj[i])
        yp, kp, vp, kip, _ = mixer_layer(yp, p_prompt[i], pos_p, sparse_attn_prompt, *w)
        attn_s = functools.partial(sparse_attn_sample, cache_k=cache_k[i], cache_v=cache_v[i],
                                   cache_kidx=cache_kidx[i], page_table=page_table)
        ys, ks_, vs_, kis, vas = mixer_layer(ys, p_sample[i], pos_s, attn_s, *w)
        kp_l.append(kp); vp_l.append(vp); kip_l.append(kip)
        ks_l.append(ks_); vs_l.append(vs_); kis_l.append(kis); vas_l.append(vas)
    k_prompt = jnp.stack(kp_l)
    v_prompt = jnp.stack(vp_l)
    kidx_prompt = jnp.stack(kip_l)
    k_sample = jnp.stack(ks_l)
    v_sample = jnp.stack(vs_l)
    kidx_sample = jnp.stack(kis_l)
    va_sample = jnp.stack(vas_l)
    return (yp, ys, k_prompt, v_prompt, kidx_prompt, k_sample, v_sample, kidx_sample, va_sample)
```

```python
import functools

import jax
import jax.numpy as jnp
import numpy as np
from jax import lax
from jax.experimental import pallas as pl
from jax.experimental.pallas import tpu as pltpu

F32 = jnp.float32
BF16 = jnp.bfloat16
I32 = jnp.int32

LANES = 128
PAGE_SIZE = 128
CHUNK = 128
TOPK_MAX = 256
ROPE_THETA = 10000.0
EPS = 1e-6
NEG = -1e30
INT_MIN = -(2 ** 31)
VMEM_LIMIT = 56 * 1024 * 1024

_NT = (((1,), (1,)), ((), ()))


def _cparams(sem):
    return pltpu.CompilerParams(dimension_semantics=sem, vmem_limit_bytes=VMEM_LIMIT)


def _sigmoid(x):
    return 1.0 / (1.0 + jnp.exp(-x))


def _pick(n, cands):
    for c in cands:
        if n % c == 0:
            return c
    return n


def _rmsnorm_kernel(x_ref, g_ref, o_ref):
    x = x_ref[...]
    ms = jnp.mean(x * x, axis=-1, keepdims=True)
    o_ref[...] = (x * lax.rsqrt(ms + EPS) * g_ref[...]).astype(o_ref.dtype)


def _rmsnorm(x, g, name):
    m, d = x.shape
    tm = _pick(m, (256, 128))
    return pl.pallas_call(
        _rmsnorm_kernel,
        grid=(m // tm,),
        in_specs=[pl.BlockSpec((tm, d), lambda i: (i, 0)),
                  pl.BlockSpec((1, d), lambda i: (0, 0))],
        out_specs=pl.BlockSpec((tm, d), lambda i: (i, 0)),
        out_shape=jax.ShapeDtypeStruct((m, d), BF16),
        compiler_params=_cparams(("parallel",)),
        name=name,
    )(x, g.reshape(1, d))


def _rope(x, cos, sin_signed):
    return x * cos + pltpu.roll(x, LANES // 2, axis=1) * sin_signed


def _proj_kernel(*refs, kind, n_f32, n_bf16, tn, scale):
    h_ref, w_ref = refs[0], refs[1]
    acc = jnp.dot(h_ref[...], w_ref[...], preferred_element_type=F32)
    pos = 2
    if kind == "norm_rope":
        g_ref, cos_ref, sin_ref = refs[pos:pos + 3]
        pos += 3
    elif kind in ("rope", "kiwi"):
        cos_ref, sin_ref = refs[pos:pos + 2]
        pos += 2
    outs = refs[pos:]
    if kind == "kiwi":
        ki = _rope(acc[:, :LANES], cos_ref[...], sin_ref[...])
        outs[0][...] = ki
        outs[1][...] = ki.astype(BF16)
        outs[2][...] = acc[:, LANES:] * scale
        return
    for j in range(tn // LANES):
        sl = slice(j * LANES, (j + 1) * LANES)
        x = acc[:, sl]
        if kind == "norm_rope":
            ms = jnp.mean(x * x, axis=-1, keepdims=True)
            x = x * lax.rsqrt(ms + EPS) * g_ref[:, sl]
            x = _rope(x, cos_ref[...], sin_ref[...])
        elif kind == "rope":
            x = _rope(x, cos_ref[...], sin_ref[...])
        elif kind == "silu":
            x = x * _sigmoid(x)
        k = 0
        for _ in range(n_f32):
            outs[k][:, sl] = x
            k += 1
        for _ in range(n_bf16):
            outs[k][:, sl] = x.astype(BF16)
            k += 1


def _proj(h, w, *, kind, n_f32, n_bf16, name, gain=None, cos=None, sin=None, scale=1.0):
    m, kdim = h.shape
    n = w.shape[1]
    tm = _pick(m, (1024, 512, 256, 128))
    tn = 256 if kind == "kiwi" else _pick(n, (512, 256, 128))
    in_specs = [pl.BlockSpec((tm, kdim), lambda i, j: (i, 0)),
                pl.BlockSpec((kdim, tn), lambda i, j: (0, j))]
    args = [h, w]
    if kind == "norm_rope":
        in_specs.append(pl.BlockSpec((1, tn), lambda i, j: (0, j)))
        args.append(gain)
    if kind in ("norm_rope", "rope", "kiwi"):
        in_specs += [pl.BlockSpec((tm, LANES), lambda i, j: (i, 0))] * 2
        args += [cos, sin]
    if kind == "kiwi":
        out_shape = [jax.ShapeDtypeStruct((m, LANES), F32), jax.ShapeDtypeStruct((m, LANES), BF16),
                     jax.ShapeDtypeStruct((m, LANES), F32)]
        out_specs = [pl.BlockSpec((tm, LANES), lambda i, j: (i, 0))] * 3
    else:
        out_shape = ([jax.ShapeDtypeStruct((m, n), F32)] * n_f32 + [jax.ShapeDtypeStruct((m, n), BF16)] * n_bf16)
        out_specs = [pl.BlockSpec((tm, tn), lambda i, j: (i, j))] * (n_f32 + n_bf16)
    return pl.pallas_call(
        functools.partial(_proj_kernel, kind=kind, n_f32=n_f32, n_bf16=n_bf16, tn=tn, scale=scale),
        grid=(m // tm, n // tn),
        in_specs=in_specs,
        out_specs=out_specs,
        out_shape=out_shape,
        compiler_params=_cparams(("parallel", "arbitrary")),
        name=name,
    )(*args)


def _branch_a_kernel(h_ref, w_ref, gva_ref, bva_ref, ws_ref, bs_ref, *rest, tm, tn, d_a, emit_vn):
    if emit_vn:
        a_ref, vn_ref, scr = rest
    else:
        a_ref, scr = rest
    n = pl.program_id(1)
    nseg = d_a // tn
    per = tn // LANES
    scr[n] = jnp.dot(h_ref[...], w_ref[...], preferred_element_type=F32)

    @pl.when(n == 3 * nseg - 1)
    def _():
        s1 = jnp.zeros((tm, 1), F32)
        for j in range(nseg):
            s1 = s1 + jnp.sum(scr[nseg + j], axis=-1, keepdims=True)
        mu = s1 / d_a
        s2 = jnp.zeros((tm, 1), F32)
        for j in range(nseg):
            dlt = scr[nseg + j] - mu
            s2 = s2 + jnp.sum(dlt * dlt, axis=-1, keepdims=True)
        rstd = lax.rsqrt(s2 / d_a + EPS)
        for g in range(d_a // LANES):
            t_idx, sl = g // per, slice((g % per) * LANES, (g % per + 1) * LANES)
            gsl = slice(g * LANES, (g + 1) * LANES)
            vn = (scr[nseg + t_idx][:, sl] - mu) * rstd * gva_ref[:, gsl] + bva_ref[:, gsl]
            if emit_vn:
                vn_ref[:, gsl] = vn
            vb = vn.astype(BF16)
            u = scr[t_idx][:, sl]
            z = scr[2 * nseg + t_idx][:, sl]
            for c in range(tm // CHUNK):
                rs = slice(c * CHUNK, (c + 1) * CHUNK)
                mixed = jnp.dot(ws_ref[g], vb[rs], preferred_element_type=F32) + bs_ref[g]
                a_ref[rs, gsl] = (u[rs] * mixed * (z[rs] * _sigmoid(z[rs]))).astype(BF16)


def _branch_a(h, w_a, g_va, b_va, ws, bs, *, emit_vn, name):
    m, kdim = h.shape
    d_a = w_a.shape[1] // 3
    tm = _pick(m, (512, 256, 128))
    tn = _pick(d_a, (512, 256, 128))
    ng = d_a // LANES
    out_shape = [jax.ShapeDtypeStruct((m, d_a), BF16)]
    out_specs = [pl.BlockSpec((tm, d_a), lambda i, j: (i, 0))]
    if emit_vn:
        out_shape.append(jax.ShapeDtypeStruct((m, d_a), F32))
        out_specs.append(pl.BlockSpec((tm, d_a), lambda i, j: (i, 0)))
    return pl.pallas_call(
        functools.partial(_branch_a_kernel, tm=tm, tn=tn, d_a=d_a, emit_vn=emit_vn),
        grid=(m // tm, 3 * d_a // tn),
        in_specs=[pl.BlockSpec((tm, kdim), lambda i, j: (i, 0)),
                  pl.BlockSpec((kdim, tn), lambda i, j: (0, j)),
                  pl.BlockSpec((1, d_a), lambda i, j: (0, 0)),
                  pl.BlockSpec((1, d_a), lambda i, j: (0, 0)),
                  pl.BlockSpec((ng, CHUNK, CHUNK), lambda i, j: (0, 0, 0)),
                  pl.BlockSpec((ng, CHUNK, LANES), lambda i, j: (0, 0, 0))],
        out_specs=out_specs,
        out_shape=out_shape,
        scratch_shapes=[pltpu.VMEM((3 * d_a // tn, tm, tn), F32)],
        compiler_params=_cparams(("parallel", "arbitrary")),
        name=name,
    )(h, w_a, g_va.reshape(1, d_a), b_va.reshape(1, d_a), ws, bs)


def _sortable_key(score):
    bits = lax.bitcast_convert_type(score, I32)
    return jnp.where(bits < 0, (bits ^ 0x7FFFFFFF) + 1, bits)


def _kth_largest_key(count_ge, rows, topk):
    def body(b, lo):
        cand = lo + lax.shift_left(jnp.int32(1), jnp.int32(31) - b)
        return jnp.where(count_ge(cand) >= topk, cand, lo)
    return lax.fori_loop(0, 32, body, jnp.full((rows, 1), INT_MIN, I32))


def _tie_index_cut(count_eq_below, need, rows, nbits):
    def body(b, ans):
        cand = ans + lax.shift_left(jnp.int32(1), jnp.int32(nbits - 1) - b)
        return jnp.where(count_eq_below(cand) < need, cand, ans)
    return lax.fori_loop(0, nbits, body, jnp.zeros((rows, 1), I32))


def _idx_kernel(qi_ref, wi_ref, ki_ref, mask_ref, key_scr, wb_scr, *, tq, kc, n_idx, n_chunks_total, topk, nbits):
    i = pl.program_id(0)
    nch = i + 1
    per = kc // LANES
    for h in range(n_idx):
        wb_scr[h] = jnp.broadcast_to(wi_ref[:, h:h + 1], (tq, LANES))
    row_t = i * tq + lax.broadcasted_iota(I32, (tq, 1), 0)

    def score_chunk(c, carry):
        kic = ki_ref[pl.ds(pl.multiple_of(c * kc, kc), kc), :]
        acc = [jnp.zeros((tq, LANES), F32) for _ in range(per)]
        for h in range(n_idx):
            d = lax.dot_general(qi_ref[:, h * LANES:(h + 1) * LANES], kic, _NT, preferred_element_type=F32)
            w = wb_scr[h]
            for j in range(per):
                acc[j] = acc[j] + jnp.maximum(d[:, j * LANES:(j + 1) * LANES], 0.0) * w
        score = jnp.concatenate(acc, axis=1)
        col = c * kc + lax.broadcasted_iota(I32, (1, kc), 1)
        key_scr[c] = jnp.where(col <= row_t, _sortable_key(score), INT_MIN)
        return carry

    lax.fori_loop(0, nch, score_chunk, 0)

    def count(pred):
        def body(c, cnt):
            p = pred(key_scr[c], c)
            for j in range(per):
                cnt = cnt + jnp.where(p[:, j * LANES:(j + 1) * LANES], 1.0, 0.0)
            return cnt
        cnt = lax.fori_loop(0, nch, body, jnp.zeros((tq, LANES), F32))
        return jnp.sum(cnt, axis=-1, keepdims=True)

    thr = _kth_largest_key(lambda cand: count(lambda k, c: k >= cand), tq, topk)
    n_ge = count(lambda k, c: k >= thr)
    tie = jnp.logical_and(n_ge > topk, thr > INT_MIN)
    thr_valid = jnp.maximum(thr, INT_MIN + 1)
    has_tie = jnp.max(jnp.where(tie, 1.0, 0.0)) > 0.0

    def write_fast():
        def body(c, carry):
            mask_ref[0, c] = jnp.where(key_scr[c] >= thr_valid, 0.0, NEG).astype(BF16)
            return carry
        lax.fori_loop(0, nch, body, 0)

    def write_ties():
        n_gt = count(lambda k, c: k > thr)
        need = topk - n_gt

        def eq_below(cand):
            return count(lambda k, c: jnp.logical_and(
                k == thr, c * kc + lax.broadcasted_iota(I32, (1, kc), 1) < cand))
        cut = _tie_index_cut(eq_below, need, tq, nbits)
        cut = jnp.where(tie, cut, jnp.int32(2 ** 30))

        def body(c, carry):
            k = key_scr[c]
            col = c * kc + lax.broadcasted_iota(I32, (1, kc), 1)
            sel = jnp.logical_and(k >= thr_valid, jnp.logical_or(k > thr, col <= cut))
            mask_ref[0, c] = jnp.where(sel, 0.0, NEG).astype(BF16)
            return carry
        lax.fori_loop(0, nch, body, 0)

    lax.cond(has_tie, write_ties, write_fast)

    def fill(c, carry):
        mask_ref[0, c] = jnp.full((tq, kc), NEG, BF16)
        return carry
    lax.fori_loop(nch, n_chunks_total, fill, 0)


def _prompt_select(qi, wi, ki, *, n_idx, name):
    s = qi.shape[0]
    tq = kc = _pick(s, (256, 128))
    nq = s // tq
    topk = min(TOPK_MAX, s // 4)
    nbits = max(1, int(np.ceil(np.log2(s))))
    return pl.pallas_call(
        functools.partial(_idx_kernel, tq=tq, kc=kc, n_idx=n_idx, n_chunks_total=nq, topk=topk, nbits=nbits),
        grid=(nq,),
        in_specs=[pl.BlockSpec((tq, n_idx * LANES), lambda i: (i, 0)),
                  pl.BlockSpec((tq, LANES), lambda i: (i, 0)),
                  pl.BlockSpec((s, LANES), lambda i: (0, 0))],
        out_specs=pl.BlockSpec((1, nq, tq, kc), lambda i: (i, 0, 0, 0)),
        out_shape=jax.ShapeDtypeStruct((nq, nq, tq, kc), BF16),
        scratch_shapes=[pltpu.VMEM((nq, tq, kc), I32), pltpu.VMEM((n_idx, tq, LANES), F32)],
        compiler_params=_cparams(("parallel",)),
        name=name,
    )(qi, wi, ki)


def _attn_kernel(q_ref, k_ref, v_ref, mask_ref, o_ref, *, tq, kc, hps, scale):
    i = pl.program_id(1)

    def body(c, carry):
        bias = mask_ref[0, c].astype(F32)
        rows = pl.ds(pl.multiple_of(c * kc, kc), kc)
        kblk = k_ref[rows, :]
        vblk = v_ref[rows, :]
        new = []
        for hh in range(hps):
            m, l, a = carry[3 * hh:3 * hh + 3]
            sl = slice(hh * LANES, (hh + 1) * LANES)
            s = lax.dot_general(q_ref[:, sl], kblk[:, sl], _NT, preferred_element_type=F32) * scale + bias
            m_new = jnp.maximum(m, jnp.max(s, axis=-1, keepdims=True))
            alpha = jnp.exp(m - m_new)
            p = jnp.exp(s - m_new)
            l = alpha * l + jnp.sum(p, axis=-1, keepdims=True)
            a = alpha * a + jnp.dot(p.astype(BF16), vblk[:, sl], preferred_element_type=F32)
            new += [m_new, l, a]
        return tuple(new)

    init = []
    for _ in range(hps):
        init += [jnp.full((tq, 1), NEG, F32), jnp.zeros((tq, 1), F32), jnp.zeros((tq, LANES), F32)]
    res = lax.fori_loop(0, i + 1, body, tuple(init))
    for hh in range(hps):
        m, l, a = res[3 * hh:3 * hh + 3]
        o_ref[:, hh * LANES:(hh + 1) * LANES] = (a / l).astype(o_ref.dtype)


def _prompt_attention(q, k, v, mask, *, name):
    s, d_b = q.shape
    nq, _, tq, kc = mask.shape
    hps = 2 if (d_b // LANES) % 2 == 0 else 1
    wblk = hps * LANES
    return pl.pallas_call(
        functools.partial(_attn_kernel, tq=tq, kc=kc, hps=hps, scale=LANES ** -0.5),
        grid=(d_b // wblk, nq),
        in_specs=[pl.BlockSpec((tq, wblk), lambda h, i: (i, h)),
                  pl.BlockSpec((s, wblk), lambda h, i: (0, h)),
                  pl.BlockSpec((s, wblk), lambda h, i: (0, h)),
                  pl.BlockSpec((1, nq, tq, kc), lambda h, i: (i, 0, 0, 0))],
        out_specs=pl.BlockSpec((tq, wblk), lambda h, i: (i, h)),
        out_shape=jax.ShapeDtypeStruct((s, d_b), BF16),
        compiler_params=_cparams(("parallel", "arbitrary")),
        name=name,
    )(q, k, v, mask)


def _sample_idx_kernel(pt_ref, *refs, g_pages, n_steps, n_idx, dec_seq, topk, nbits, past):
    del pt_ref
    pages = refs[:g_pages]
    qi_ref, wcol_ref, kinew_ref, bias_past_ref, bias_new_ref, kb_scr, key_scr, keyn_scr = refs[g_pages:]
    j = pl.program_id(1)
    pg = g_pages * PAGE_SIZE
    rows = 8
    row_id = lax.broadcasted_iota(I32, (rows, 1), 0)

    def scores_of(kmat):
        d = lax.dot_general(qi_ref[0], kmat, _NT, preferred_element_type=F32)
        wr = jnp.maximum(d, 0.0) * wcol_ref[0]
        s4 = jnp.sum(wr.reshape(dec_seq, n_idx, kmat.shape[0]), axis=1)
        return jnp.concatenate([s4, jnp.zeros((rows - dec_seq, kmat.shape[0]), F32)], axis=0)

    for r in range(g_pages):
        kb_scr[r * PAGE_SIZE:(r + 1) * PAGE_SIZE, :] = pages[r][0].astype(BF16)
    key_scr[j] = jnp.where(row_id < dec_seq, _sortable_key(scores_of(kb_scr[...])), INT_MIN)

    @pl.when(j == n_steps - 1)
    def _():
        sn = scores_of(kinew_ref[0])
        coln = lax.broadcasted_iota(I32, (1, LANES), 1)
        okn = jnp.logical_and(jnp.logical_and(coln < dec_seq, coln <= row_id), row_id < dec_seq)
        keyn_scr[...] = jnp.where(okn, _sortable_key(sn), INT_MIN)

        def count(pred):
            cnt = jnp.zeros((rows, LANES), F32)
            for c in range(n_steps):
                p = pred(key_scr[c], c * pg + lax.broadcasted_iota(I32, (1, pg), 1))
                for jj in range(pg // LANES):
                    cnt = cnt + jnp.where(p[:, jj * LANES:(jj + 1) * LANES], 1.0, 0.0)
            cnt = cnt + jnp.where(pred(keyn_scr[...], past + coln), 1.0, 0.0)
            return jnp.sum(cnt, axis=-1, keepdims=True)

        thr = _kth_largest_key(lambda cand: count(lambda k, col: k >= cand), rows, topk)
        n_ge = count(lambda k, col: k >= thr)
        n_gt = count(lambda k, col: k > thr)
        tie = jnp.logical_and(n_ge > topk, thr > INT_MIN)
        thr_valid = jnp.maximum(thr, INT_MIN + 1)
        need = topk - n_gt
        cut = _tie_index_cut(
            lambda cand: count(lambda k, col: jnp.logical_and(k == thr, col < cand)), need, rows, nbits)
        cut = jnp.where(tie, cut, jnp.int32(2 ** 30))

        def sel(k, col):
            return jnp.logical_and(k >= thr_valid, jnp.logical_or(k > thr, col <= cut))

        half = pg // 2
        for c in range(n_steps):
            b = jnp.where(sel(key_scr[c], c * pg + lax.broadcasted_iota(I32, (1, pg), 1)), 0.0, NEG)
            bias_past_ref[0, 2 * c] = b[:, :half]
            bias_past_ref[0, 2 * c + 1] = b[:, half:]
        bias_new_ref[0] = jnp.where(sel(keyn_scr[...], past + coln), 0.0, NEG)


def _sample_select(page_table, cache_kidx, qi_rows, wcol, kinew, *, n_idx, dec_seq, name):
    bd, n_pages = page_table.shape
    g_pages = _pick(n_pages, (8, 4, 2, 1))
    n_steps = n_pages // g_pages
    pg = g_pages * PAGE_SIZE
    past = n_pages * PAGE_SIZE
    topk = min(TOPK_MAX, (past + dec_seq) // 4)
    nbits = max(1, int(np.ceil(np.log2(past + LANES))))
    rq = dec_seq * n_idx

    def page_spec(r):
        return pl.BlockSpec((1, PAGE_SIZE, LANES), lambda b, j, pt: (pt[b, j * g_pages + r], 0, 0))

    grid_spec = pltpu.PrefetchScalarGridSpec(
        num_scalar_prefetch=1,
        grid=(bd, n_steps),
        in_specs=[page_spec(r) for r in range(g_pages)] + [
            pl.BlockSpec((1, rq, LANES), lambda b, j, pt: (b, 0, 0)),
            pl.BlockSpec((1, rq, 1), lambda b, j, pt: (b, 0, 0)),
            pl.BlockSpec((1, LANES, LANES), lambda b, j, pt: (b, 0, 0))],
        out_specs=[pl.BlockSpec((1, 2 * n_steps, 8, pg // 2), lambda b, j, pt: (b, 0, 0, 0)),
                   pl.BlockSpec((1, 8, LANES), lambda b, j, pt: (b, 0, 0))],
        scratch_shapes=[pltpu.VMEM((pg, LANES), BF16), pltpu.VMEM((n_steps, 8, pg), I32),
                        pltpu.VMEM((8, LANES), I32)],
    )
    return pl.pallas_call(
        functools.partial(_sample_idx_kernel, g_pages=g_pages, n_steps=n_steps, n_idx=n_idx,
                          dec_seq=dec_seq, topk=topk, nbits=nbits, past=past),
        grid_spec=grid_spec,
        out_shape=[jax.ShapeDtypeStruct((bd, 2 * n_steps, 8, pg // 2), F32),
                   jax.ShapeDtypeStruct((bd, 8, LANES), F32)],
        compiler_params=_cparams(("parallel", "arbitrary")),
        name=name,
    )(page_table, *([cache_kidx] * g_pages), qi_rows, wcol, kinew)


def _sample_attn_kernel(pt_ref, *refs, g_pages, n_steps, n_heads, scale):
    del pt_ref
    kpages = refs[:g_pages]
    vpages = refs[g_pages:2 * g_pages]
    (qblk_ref, bias_ref, biasn_ref, knew_ref, vnew_ref, o_ref,
     kb_scr, vb_scr, m_scr, l_scr, acc_scr) = refs[2 * g_pages:]
    j = pl.program_id(1)
    rows = 8 * n_heads

    @pl.when(j == 0)
    def _():
        m_scr[...] = jnp.full(m_scr.shape, NEG, F32)
        l_scr[...] = jnp.zeros(l_scr.shape, F32)
        acc_scr[...] = jnp.zeros(acc_scr.shape, F32)

    def update(kb, vb, bias8):
        st = jnp.dot(kb, qblk_ref[0], preferred_element_type=F32)
        s = st.T * scale + jnp.concatenate([bias8] * n_heads, axis=0)
        m = m_scr[...]
        m_new = jnp.maximum(m, jnp.max(s, axis=-1, keepdims=True))
        alpha = jnp.exp(m - m_new)
        p = jnp.exp(s - m_new)
        l_scr[...] = alpha * l_scr[...] + jnp.sum(p, axis=-1, keepdims=True)
        acc_scr[...] = alpha * acc_scr[...] + jnp.dot(p.astype(BF16), vb, preferred_element_type=F32)
        m_scr[...] = m_new

    for r in range(g_pages):
        kb_scr[r * PAGE_SIZE:(r + 1) * PAGE_SIZE, :] = kpages[r][0].astype(BF16)
        vb_scr[r * PAGE_SIZE:(r + 1) * PAGE_SIZE, :] = vpages[r][0].astype(BF16)
    update(kb_scr[...], vb_scr[...], bias_ref[0, 0])

    @pl.when(j == n_steps - 1)
    def _():
        update(knew_ref[0], vnew_ref[0], biasn_ref[0])
        for h in range(n_heads):
            rs = slice(h * 8, (h + 1) * 8)
            cs = slice(h * LANES, (h + 1) * LANES)
            o_ref[0, :, cs] = acc_scr[rs, cs] / l_scr[rs, :]


def _sample_attention(page_table, cache_k, cache_v, qblk, bias_past, bias_new, knew, vnew, *, n_heads, name):
    bd, n_pages = page_table.shape
    d_b = cache_k.shape[-1]
    n_steps = bias_past.shape[1]
    g_pages = n_pages // n_steps
    pg = g_pages * PAGE_SIZE
    rows = 8 * n_heads

    def page_spec(r):
        return pl.BlockSpec((1, PAGE_SIZE, d_b), lambda b, j, pt: (pt[b, j * g_pages + r], 0, 0))

    grid_spec = pltpu.PrefetchScalarGridSpec(
        num_scalar_prefetch=1,
        grid=(bd, n_steps),
        in_specs=[page_spec(r) for r in range(g_pages)] * 2 + [
            pl.BlockSpec((1, d_b, rows), lambda b, j, pt: (b, 0, 0)),
            pl.BlockSpec((1, 1, 8, pg), lambda b, j, pt: (b, j, 0, 0)),
            pl.BlockSpec((1, 8, LANES), lambda b, j, pt: (b, 0, 0)),
            pl.BlockSpec((1, LANES, d_b), lambda b, j, pt: (b, 0, 0)),
            pl.BlockSpec((1, LANES, d_b), lambda b, j, pt: (b, 0, 0))],
        out_specs=pl.BlockSpec((1, 8, d_b), lambda b, j, pt: (b, 0, 0)),
        scratch_shapes=[pltpu.VMEM((pg, d_b), BF16), pltpu.VMEM((pg, d_b), BF16),
                        pltpu.VMEM((rows, 1), F32), pltpu.VMEM((rows, 1), F32), pltpu.VMEM((rows, d_b), F32)],
    )
    return pl.pallas_call(
        functools.partial(_sample_attn_kernel, g_pages=g_pages, n_steps=n_steps, n_heads=n_heads,
                          scale=LANES ** -0.5),
        grid_spec=grid_spec,
        out_shape=jax.ShapeDtypeStruct((bd, 8, d_b), F32),
        compiler_params=_cparams(("parallel", "arbitrary")),
        name=name,
    )(page_table, *([cache_k] * g_pages), *([cache_v] * g_pages), qblk, bias_past, bias_new, knew, vnew)


def _merge_kernel(h_ref, a_ref, attn_ref, zs_ref, wga_ref, wgb_ref, wua_ref, wub_ref, o_ref, bg_scr):
    @pl.when(pl.program_id(1) == 0)
    def _():
        bg_scr[...] = (attn_ref[...].astype(F32) * zs_ref[...].astype(F32)).astype(BF16)

    h = h_ref[...]
    ga = jnp.dot(h, wga_ref[...], preferred_element_type=F32)
    gb = jnp.dot(h, wgb_ref[...], preferred_element_type=F32)
    a_up = jnp.dot(a_ref[...], wua_ref[...], preferred_element_type=F32)
    b_up = jnp.dot(bg_scr[...], wub_ref[...], preferred_element_type=F32)
    o_ref[...] = (_sigmoid(ga) * a_up + _sigmoid(gb) * b_up).astype(o_ref.dtype)


def _merge(h, a_gated, attn, zs, w_ga, w_gb, w_ua, w_ub, *, name):
    m, d = h.shape
    d_a, d_b = a_gated.shape[1], attn.shape[1]
    tm = _pick(m, (512, 256, 128))
    tn = _pick(d, (256, 128))
    row = lambda i, j: (i, 0)
    col = lambda i, j: (0, j)
    return pl.pallas_call(
        _merge_kernel,
        grid=(m // tm, d // tn),
        in_specs=[pl.BlockSpec((tm, d), row), pl.BlockSpec((tm, d_a), row),
                  pl.BlockSpec((tm, d_b), row), pl.BlockSpec((tm, d_b), row),
                  pl.BlockSpec((d, tn), col), pl.BlockSpec((d, tn), col),
                  pl.BlockSpec((d_a, tn), col), pl.BlockSpec((d_b, tn), col)],
        out_specs=pl.BlockSpec((tm, tn), lambda i, j: (i, j)),
        out_shape=jax.ShapeDtypeStruct((m, d), BF16),
        scratch_shapes=[pltpu.VMEM((tm, d_b), BF16)],
        compiler_params=_cparams(("parallel", "arbitrary")),
        name=name,
    )(h, a_gated, attn, zs, w_ga, w_gb, w_ua, w_ub)


def _resid_kernel(mg_ref, w_ref, x_ref, o_ref):
    o_ref[...] = x_ref[...] + jnp.dot(mg_ref[...], w_ref[...], preferred_element_type=F32)


def _resid_proj(merged, w_o, x, *, name):
    m, d = x.shape
    tm = _pick(m, (1024, 512, 256, 128))
    tn = _pick(d, (512, 256, 128))
    return pl.pallas_call(
        _resid_kernel,
        grid=(m // tm, d // tn),
        in_specs=[pl.BlockSpec((tm, merged.shape[1]), lambda i, j: (i, 0)),
                  pl.BlockSpec((merged.shape[1], tn), lambda i, j: (0, j)),
                  pl.BlockSpec((tm, tn), lambda i, j: (i, j))],
        out_specs=pl.BlockSpec((tm, tn), lambda i, j: (i, j)),
        out_shape=jax.ShapeDtypeStruct((m, d), F32),
        compiler_params=_cparams(("parallel", "arbitrary")),
        name=name,
    )(merged, w_o, x)


def _ple_kernel(hn_ref, wg_ref, p_ref, wp_ref, x1_ref, o_ref):
    gate = _sigmoid(jnp.dot(hn_ref[...], wg_ref[...], preferred_element_type=F32))
    emb = jnp.dot(p_ref[...], wp_ref[...], preferred_element_type=F32)
    o_ref[...] = x1_ref[...] + gate * emb


def _ple(hn, w_gate, p, w_proj, x1, *, name):
    m, d = x1.shape
    tm = _pick(m, (1024, 512, 256, 128))
    tn = _pick(d, (512, 256, 128))
    return pl.pallas_call(
        _ple_kernel,
        grid=(m // tm, d // tn),
        in_specs=[pl.BlockSpec((tm, d), lambda i, j: (i, 0)),
                  pl.BlockSpec((d, tn), lambda i, j: (0, j)),
                  pl.BlockSpec((tm, p.shape[1]), lambda i, j: (i, 0)),
                  pl.BlockSpec((p.shape[1], tn), lambda i, j: (0, j)),
                  pl.BlockSpec((tm, tn), lambda i, j: (i, j))],
        out_specs=pl.BlockSpec((tm, tn), lambda i, j: (i, j)),
        out_shape=jax.ShapeDtypeStruct((m, d), F32),
        compiler_params=_cparams(("parallel", "arbitrary")),
        name=name,
    )(hn, w_gate, p, w_proj, x1)


def _rope_tables(pos):
    half = LANES // 2
    inv = ROPE_THETA ** (-jnp.arange(half, dtype=F32) / half)
    ang = pos.astype(F32)[:, None] * inv[None, :]
    cos, sin = jnp.cos(ang), jnp.sin(ang)
    return jnp.concatenate([cos, cos], axis=-1), jnp.concatenate([-sin, sin], axis=-1)


def _split_weights(w_in, d_model, d_a, d_b, n_idx):
    widths = [3 * d_a, d_b, d_b, d_b, d_b, n_idx * LANES, LANES + n_idx, d_model, d_model]
    offs = np.concatenate([[0], np.cumsum(widths)])
    parts = [w_in[:, int(offs[i]):int(offs[i + 1])].astype(BF16) for i in range(len(widths))]
    parts[6] = jnp.pad(parts[6], ((0, 0), (0, 2 * LANES - parts[6].shape[1])))
    return parts


def _layer(x, p, pos, tag, w, ws, bs, *, n_heads, n_idx, emit_vn):
    m, d_model = x.shape
    cosf, sinf = _rope_tables(pos)
    h = _rmsnorm(x, w["g_norm"], f"{tag}_norm")
    a_out = _branch_a(h, w["w_a"], w["g_va"], w["b_va"], ws, bs, emit_vn=emit_vn, name=f"{tag}_branch_a")
    gq = jnp.tile(w["g_q"], n_heads).reshape(1, -1)
    gk = jnp.tile(w["g_k"], n_heads).reshape(1, -1)
    (q_bf,) = _proj(h, w["w_q"], kind="norm_rope", n_f32=0, n_bf16=1, gain=gq, cos=cosf, sin=sinf, name=f"{tag}_q")
    k_f32, k_bf = _proj(h, w["w_k"], kind="norm_rope", n_f32=1, n_bf16=1, gain=gk, cos=cosf, sin=sinf,
                        name=f"{tag}_k")
    v_f32, v_bf = _proj(h, w["w_v"], kind="plain", n_f32=1, n_bf16=1, name=f"{tag}_v")
    (zs,) = _proj(h, w["w_zb"], kind="silu", n_f32=0, n_bf16=1, name=f"{tag}_zb")
    (qi,) = _proj(h, w["w_qi"], kind="rope", n_f32=0, n_bf16=1, cos=cosf, sin=sinf, name=f"{tag}_qi")
    ki_f32, ki_bf, wi = _proj(h, w["w_kiwi"], kind="kiwi", n_f32=0, n_bf16=0, cos=cosf, sin=sinf,
                              scale=float(n_idx ** -0.5 * LANES ** -0.5), name=f"{tag}_kiwi")
    return dict(h=h, a=a_out, q=q_bf, k=k_f32, kb=k_bf, v=v_f32, vb=v_bf, zs=zs, qi=qi, ki=ki_f32, kib=ki_bf, wi=wi)


def _finish(x, p, t, attn, tag, w):
    merged = _merge(t["h"], t["a"][0], attn, t["zs"], w["w_ga"], w["w_gb"], w["w_ua"], w["w_ub"], name=f"{tag}_merge")
    x1 = _resid_proj(merged, w["w_o"], x, name=f"{tag}_resid")
    hn = _rmsnorm(x1, w["g_ple"], f"{tag}_ple_norm")
    return _ple(hn, w["w_pg"], p.astype(BF16), w["w_pp"], x1, name=f"{tag}_ple")


def kernel(x_prompt, x_sample, p_prompt, p_sample, cache_k, cache_v, cache_kidx, page_table, g_norm, w_in, g_va,
           b_va, w_s, b_s, g_q, g_k, w_up_a, w_up_b, w_o, g_ple, w_ple_gate, w_ple_proj):
    depth = w_in.shape[0]
    batch, seq, d_model = x_prompt.shape
    bd, dec_seq, _ = x_sample.shape
    n_phys, _, n_heads, head_dim = cache_k.shape[1:]
    n_groups = w_s.shape[1]
    d_a, d_b = w_up_a.shape[1], w_up_b.shape[1]
    n_idx = w_in.shape[2] - (3 * d_a + 4 * d_b + LANES + 2 * d_model)
    n_idx = n_idx // (LANES + 1)
    n_pages = page_table.shape[1]
    past = n_pages * PAGE_SIZE
    assert batch == 1 and head_dim == LANES and cache_kidx.shape[-1] == LANES and d_a == n_groups * LANES
    assert seq % CHUNK == 0 and bd * dec_seq == CHUNK and dec_seq <= 8 and d_b == n_heads * LANES
    ms = bd * dec_seq

    yp = x_prompt.reshape(seq, d_model)
    ys = x_sample.reshape(ms, d_model)
    pos_p = jnp.arange(seq)
    pos_s = jnp.tile(past + jnp.arange(dec_seq), bd)
    outs = {k: [] for k in ("kp", "vp", "kip", "ks", "vs", "kis", "vas")}
    tril = jnp.tril(jnp.ones((CHUNK, CHUNK), bool))
    for i in range(depth):
        parts = _split_weights(w_in[i], d_model, d_a, d_b, n_idx)
        w = dict(zip(("w_a", "w_q", "w_k", "w_v", "w_zb", "w_qi", "w_kiwi", "w_ga", "w_gb"), parts))
        w.update(g_norm=g_norm[i], g_va=g_va[i], b_va=b_va[i], g_q=g_q[i], g_k=g_k[i], g_ple=g_ple[i],
                 w_ua=w_up_a[i].astype(BF16), w_ub=w_up_b[i].astype(BF16), w_o=w_o[i].astype(BF16),
                 w_pg=w_ple_gate[i].astype(BF16), w_pp=w_ple_proj[i].astype(BF16))
        ws_p = jnp.where(tril[None], w_s[i], 0.0).astype(BF16)
        bs_p = jnp.broadcast_to(b_s[i][:, :, None], (n_groups, CHUNK, LANES))
        small = jnp.where(tril[None, :dec_seq, :dec_seq], w_s[i][:, :dec_seq, :dec_seq], 0.0)
        ws_s = jnp.einsum("ab,gts->gatbs", jnp.eye(bd, dtype=F32), small).reshape(n_groups, ms, ms).astype(BF16)
        bs_s = jnp.broadcast_to(jnp.tile(b_s[i][:, :dec_seq], (1, bd))[:, :, None], (n_groups, ms, LANES))

        tp = _layer(yp, p_prompt[i, 0], pos_p, "prompt", w, ws_p, bs_p, n_heads=n_heads, n_idx=n_idx, emit_vn=False)
        mask = _prompt_select(tp["qi"], tp["wi"], tp["kib"], n_idx=n_idx, name="prompt_select")
        attn_p = _prompt_attention(tp["q"], tp["kb"], tp["vb"], mask, name="prompt_attention")
        yp_new = _finish(yp, p_prompt[i, 0], tp, attn_p, "prompt", w)

        tsm = _layer(ys, p_sample[i].reshape(ms, -1), pos_s, "sample", w, ws_s, bs_s, n_heads=n_heads, n_idx=n_idx,
                     emit_vn=True)
        qi_rows = tsm["qi"].reshape(bd, dec_seq * n_idx, LANES)
        wcol = tsm["wi"][:, :n_idx].reshape(bd, dec_seq * n_idx, 1)
        kinew = jnp.pad(tsm["kib"].reshape(bd, dec_seq, LANES), ((0, 0), (0, LANES - dec_seq), (0, 0)))
        bias_past, bias_new = _sample_select(page_table, cache_kidx[i], qi_rows, wcol, kinew, n_idx=n_idx,
                                             dec_seq=dec_seq, name="sample_select")
        q4 = jnp.pad(tsm["q"].reshape(bd, dec_seq, n_heads, LANES), ((0, 0), (0, 8 - dec_seq), (0, 0), (0, 0)))
        qblk = jnp.einsum("bthd,hg->bhdgt", q4, jnp.eye(n_heads, dtype=BF16)).reshape(bd, d_b, n_heads * 8)
        padrows = ((0, 0), (0, LANES - dec_seq), (0, 0))
        knew = jnp.pad(tsm["kb"].reshape(bd, dec_seq, d_b), padrows)
        vnew = jnp.pad(tsm["vb"].reshape(bd, dec_seq, d_b), padrows)
        attn_s8 = _sample_attention(page_table, cache_k[i].reshape(n_phys, PAGE_SIZE, d_b),
                                    cache_v[i].reshape(n_phys, PAGE_SIZE, d_b), qblk, bias_past, bias_new,
                                    knew, vnew, n_heads=n_heads, name="sample_attention")
        attn_s = attn_s8[:, :dec_seq].reshape(ms, d_b).astype(BF16)
        ys_new = _finish(ys, p_sample[i].reshape(ms, -1), tsm, attn_s, "sample", w)

        outs["kp"].append(tp["k"].reshape(batch, seq, n_heads, LANES))
        outs["vp"].append(tp["v"].reshape(batch, seq, n_heads, LANES))
        outs["kip"].append(tp["ki"].reshape(batch, seq, LANES))
        outs["ks"].append(tsm["k"].reshape(bd, dec_seq, n_heads, LANES))
        outs["vs"].append(tsm["v"].reshape(bd, dec_seq, n_heads, LANES))
        outs["kis"].append(tsm["ki"].reshape(bd, dec_seq, LANES))
        outs["vas"].append(tsm["a"][1].reshape(bd, dec_seq, d_a))
        yp, ys = yp_new, ys_new

    return (yp.reshape(batch, seq, d_model), ys.reshape(bd, dec_seq, d_model),
            jnp.stack(outs["kp"]), jnp.stack(outs["vp"]), jnp.stack(outs["kip"]),
            jnp.stack(outs["ks"]), jnp.stack(outs["vs"]), jnp.stack(outs["kis"]), jnp.stack(outs["vas"]))
```

```python
import functools

import jax
import jax.numpy as jnp
import numpy as np
from jax import lax
from jax.experimental import pallas as pl
from jax.experimental.pallas import tpu as pltpu

F32 = jnp.float32
BF16 = jnp.bfloat16
I32 = jnp.int32

LANES = 128
PAGE_SIZE = 128
CHUNK = 128
TOPK_MAX = 256
ROPE_THETA = 10000.0
EPS = 1e-6
NEG = -1e30
INT_MIN = -(2 ** 31)
VMEM_LIMIT = 56 * 1024 * 1024

_NT = (((1,), (1,)), ((), ()))


def _cparams(sem):
    return pltpu.CompilerParams(dimension_semantics=sem, vmem_limit_bytes=VMEM_LIMIT)


def _sigmoid(x):
    return 1.0 / (1.0 + jnp.exp(-x))


def _pick(n, cands):
    for c in cands:
        if n % c == 0:
            return c
    return n


def _rmsnorm_kernel(x_ref, g_ref, o_ref):
    x = x_ref[...]
    ms = jnp.mean(x * x, axis=-1, keepdims=True)
    o_ref[...] = (x * lax.rsqrt(ms + EPS) * g_ref[...]).astype(o_ref.dtype)


def _rmsnorm(x, g, name):
    m, d = x.shape
    tm = _pick(m, (256, 128))
    return pl.pallas_call(
        _rmsnorm_kernel,
        grid=(m // tm,),
        in_specs=[pl.BlockSpec((tm, d), lambda i: (i, 0)),
                  pl.BlockSpec((1, d), lambda i: (0, 0))],
        out_specs=pl.BlockSpec((tm, d), lambda i: (i, 0)),
        out_shape=jax.ShapeDtypeStruct((m, d), BF16),
        compiler_params=_cparams(("parallel",)),
        name=name,
    )(x, g.reshape(1, d))


def _rope(x, cos, sin_signed):
    return x * cos + pltpu.roll(x, LANES // 2, axis=1) * sin_signed


def _proj_kernel(*refs, kind, n_f32, n_bf16, tn, scale):
    h_ref, w_ref = refs[0], refs[1]
    acc = jnp.dot(h_ref[...], w_ref[...], preferred_element_type=F32)
    pos = 2
    if kind == "norm_rope":
        g_ref, cos_ref, sin_ref = refs[pos:pos + 3]
        pos += 3
    elif kind in ("rope", "kiwi"):
        cos_ref, sin_ref = refs[pos:pos + 2]
        pos += 2
    outs = refs[pos:]
    if kind == "kiwi":
        ki = _rope(acc[:, :LANES], cos_ref[...], sin_ref[...])
        outs[0][...] = ki
        outs[1][...] = ki.astype(BF16)
        outs[2][...] = acc[:, LANES:] * scale
        return
    for j in range(tn // LANES):
        sl = slice(j * LANES, (j + 1) * LANES)
        x = acc[:, sl]
        if kind == "norm_rope":
            ms = jnp.mean(x * x, axis=-1, keepdims=True)
            x = x * lax.rsqrt(ms + EPS) * g_ref[:, sl]
            x = _rope(x, cos_ref[...], sin_ref[...])
        elif kind == "rope":
            x = _rope(x, cos_ref[...], sin_ref[...])
        elif kind == "silu":
            x = x * _sigmoid(x)
        k = 0
        for _ in range(n_f32):
            outs[k][:, sl] = x
            k += 1
        for _ in range(n_bf16):
            outs[k][:, sl] = x.astype(BF16)
            k += 1


def _proj(h, w, *, kind, n_f32, n_bf16, name, gain=None, cos=None, sin=None, scale=1.0):
    m, kdim = h.shape
    n = w.shape[1]
    tm = _pick(m, (1024, 512, 256, 128))
    tn = 256 if kind == "kiwi" else _pick(n, (512, 256, 128))
    in_specs = [pl.BlockSpec((tm, kdim), lambda i, j: (i, 0)),
                pl.BlockSpec((kdim, tn), lambda i, j: (0, j))]
    args = [h, w]
    if kind == "norm_rope":
        in_specs.append(pl.BlockSpec((1, tn), lambda i, j: (0, j)))
        args.append(gain)
    if kind in ("norm_rope", "rope", "kiwi"):
        in_specs += [pl.BlockSpec((tm, LANES), lambda i, j: (i, 0))] * 2
        args += [cos, sin]
    if kind == "kiwi":
        out_shape = [jax.ShapeDtypeStruct((m, LANES), F32), jax.ShapeDtypeStruct((m, LANES), BF16),
                     jax.ShapeDtypeStruct((m, LANES), F32)]
        out_specs = [pl.BlockSpec((tm, LANES), lambda i, j: (i, 0))] * 3
    else:
        out_shape = ([jax.ShapeDtypeStruct((m, n), F32)] * n_f32 + [jax.ShapeDtypeStruct((m, n), BF16)] * n_bf16)
        out_specs = [pl.BlockSpec((tm, tn), lambda i, j: (i, j))] * (n_f32 + n_bf16)
    return pl.pallas_call(
        functools.partial(_proj_kernel, kind=kind, n_f32=n_f32, n_bf16=n_bf16, tn=tn, scale=scale),
        grid=(m // tm, n // tn),
        in_specs=in_specs,
        out_specs=out_specs,
        out_shape=out_shape,
        compiler_params=_cparams(("parallel", "arbitrary")),
        name=name,
    )(*args)


def _branch_a_kernel(h_ref, w_ref, gva_ref, bva_ref, ws_ref, bs_ref, *rest, tm, tn, d_a, emit_vn):
    if emit_vn:
        a_ref, vn_ref, scr = rest
    else:
        a_ref, scr = rest
    n = pl.program_id(1)
    nseg = d_a // tn
    per = tn // LANES
    scr[n] = jnp.dot(h_ref[...], w_ref[...], preferred_element_type=F32)

    @pl.when(n == 3 * nseg - 1)
    def _():
        s1 = jnp.zeros((tm, 1), F32)
        for j in range(nseg):
            s1 = s1 + jnp.sum(scr[nseg + j], axis=-1, keepdims=True)
        mu = s1 / d_a
        s2 = jnp.zeros((tm, 1), F32)
        for j in range(nseg):
            dlt = scr[nseg + j] - mu
            s2 = s2 + jnp.sum(dlt * dlt, axis=-1, keepdims=True)
        rstd = lax.rsqrt(s2 / d_a + EPS)
        for g in range(d_a // LANES):
            t_idx, sl = g // per, slice((g % per) * LANES, (g % per + 1) * LANES)
            gsl = slice(g * LANES, (g + 1) * LANES)
            vn = (scr[nseg + t_idx][:, sl] - mu) * rstd * gva_ref[:, gsl] + bva_ref[:, gsl]
            if emit_vn:
                vn_ref[:, gsl] = vn
            vb = vn.astype(BF16)
            u = scr[t_idx][:, sl]
            z = scr[2 * nseg + t_idx][:, sl]
            for c in range(tm // CHUNK):
                rs = slice(c * CHUNK, (c + 1) * CHUNK)
                mixed = jnp.dot(ws_ref[g], vb[rs], preferred_element_type=F32) + bs_ref[g]
                a_ref[rs, gsl] = (u[rs] * mixed * (z[rs] * _sigmoid(z[rs]))).astype(BF16)


def _branch_a(h, w_a, g_va, b_va, ws, bs, *, emit_vn, name):
    m, kdim = h.shape
    d_a = w_a.shape[1] // 3
    tm = _pick(m, (512, 256, 128))
    tn = _pick(d_a, (512, 256, 128))
    ng = d_a // LANES
    out_shape = [jax.ShapeDtypeStruct((m, d_a), BF16)]
    out_specs = [pl.BlockSpec((tm, d_a), lambda i, j: (i, 0))]
    if emit_vn:
        out_shape.append(jax.ShapeDtypeStruct((m, d_a), F32))
        out_specs.append(pl.BlockSpec((tm, d_a), lambda i, j: (i, 0)))
    return pl.pallas_call(
        functools.partial(_branch_a_kernel, tm=tm, tn=tn, d_a=d_a, emit_vn=emit_vn),
        grid=(m // tm, 3 * d_a // tn),
        in_specs=[pl.BlockSpec((tm, kdim), lambda i, j: (i, 0)),
                  pl.BlockSpec((kdim, tn), lambda i, j: (0, j)),
                  pl.BlockSpec((1, d_a), lambda i, j: (0, 0)),
                  pl.BlockSpec((1, d_a), lambda i, j: (0, 0)),
                  pl.BlockSpec((ng, CHUNK, CHUNK), lambda i, j: (0, 0, 0)),
                  pl.BlockSpec((ng, CHUNK, LANES), lambda i, j: (0, 0, 0))],
        out_specs=out_specs,
        out_shape=out_shape,
        scratch_shapes=[pltpu.VMEM((3 * d_a // tn, tm, tn), F32)],
        compiler_params=_cparams(("parallel", "arbitrary")),
        name=name,
    )(h, w_a, g_va.reshape(1, d_a), b_va.reshape(1, d_a), ws, bs)


def _sortable_key(score):
    bits = lax.bitcast_convert_type(score, I32)
    return jnp.where(bits < 0, (bits ^ 0x7FFFFFFF) + 1, bits)


def _kth_largest_key(count_ge, rows, topk):
    n_valid = count_ge(jnp.full((rows, 1), INT_MIN + 1, I32))

    def unresolved(cnt):
        return (jnp.max(jnp.where(jnp.logical_and(cnt != topk, n_valid >= topk), 1.0, 0.0)) > 0.0).astype(I32)

    def cond(state):
        b, _, _, go = state
        return jnp.logical_and(b < 32, go > 0)

    def body(state):
        b, lo, cnt_lo, _ = state
        cand = lo + lax.shift_left(jnp.int32(1), jnp.int32(31) - b)
        cnt = count_ge(cand)
        take = cnt >= topk
        lo = jnp.where(take, cand, lo)
        cnt_lo = jnp.where(take, cnt, cnt_lo)
        return b + 1, lo, cnt_lo, unresolved(cnt_lo)

    lo0 = jnp.full((rows, 1), INT_MIN, I32)
    cnt0 = jnp.full((rows, 1), 2.0 ** 31, F32)
    _, thr, n_ge, _ = lax.while_loop(cond, body, (jnp.int32(0), lo0, cnt0, jnp.int32(1)))
    return thr, n_ge


def _tie_index_cut(count_eq_below, need, rows, nbits):
    def body(b, ans):
        cand = ans + lax.shift_left(jnp.int32(1), jnp.int32(nbits - 1) - b)
        return jnp.where(count_eq_below(cand) < need, cand, ans)
    return lax.fori_loop(0, nbits, body, jnp.zeros((rows, 1), I32))


def _idx_kernel(qi_ref, wi_ref, ki_ref, mask_ref, key_scr, wb_scr, *, tq, kc, n_idx, n_chunks_total, topk, nbits):
    i = pl.program_id(0)
    nch = i + 1
    per = kc // LANES
    for h in range(n_idx):
        wb_scr[h] = jnp.broadcast_to(wi_ref[:, h:h + 1], (tq, LANES))
    row_t = i * tq + lax.broadcasted_iota(I32, (tq, 1), 0)

    def score_chunk(c, carry):
        kic = ki_ref[pl.ds(pl.multiple_of(c * kc, kc), kc), :]
        acc = [jnp.zeros((tq, LANES), F32) for _ in range(per)]
        for h in range(n_idx):
            d = lax.dot_general(qi_ref[:, h * LANES:(h + 1) * LANES], kic, _NT, preferred_element_type=F32)
            w = wb_scr[h]
            for j in range(per):
                acc[j] = acc[j] + jnp.maximum(d[:, j * LANES:(j + 1) * LANES], 0.0) * w
        score = jnp.concatenate(acc, axis=1)
        col = c * kc + lax.broadcasted_iota(I32, (1, kc), 1)
        key_scr[c] = jnp.where(col <= row_t, _sortable_key(score), INT_MIN)
        return carry

    lax.fori_loop(0, nch, score_chunk, 0)

    rb = min(tq, 128)

    lane = lax.broadcasted_iota(I32, (1, LANES), 1)

    def per_rows(make_pred, fn, init):
        outs = []
        for r0 in range(0, tq, rb):
            rs = slice(r0, r0 + rb)
            pred = make_pred(lambda v, rs=rs: jnp.broadcast_to(v[rs], (rb, LANES)))

            def body(c, carry, rs=rs, pred=pred):
                for j in range(per):
                    k = key_scr[c, rs, j * LANES:(j + 1) * LANES]
                    carry = fn(c, carry, rs, j, pred(k, c * kc + j * LANES + lane))
                return carry
            outs.append(lax.fori_loop(0, nch, body, init))
        return outs

    def count(make_pred):
        cnts = per_rows(make_pred, lambda c, cnt, rs, j, p: cnt + jnp.where(p, 1.0, 0.0),
                        jnp.zeros((rb, LANES), F32))
        return jnp.concatenate([jnp.sum(cnt, axis=-1, keepdims=True) for cnt in cnts], axis=0)

    def write_mask(make_pred):
        def put(c, carry, rs, j, p):
            mask_ref[0, c, rs, j * LANES:(j + 1) * LANES] = jnp.where(p, 0.0, NEG).astype(BF16)
            return carry
        per_rows(make_pred, put, 0)

    def ge(v):
        return lambda bc: (lambda k, col, vb=bc(v): k >= vb)

    thr, n_ge = _kth_largest_key(lambda cand: count(ge(cand)), tq, topk)
    tie = jnp.logical_and(n_ge > topk, thr > INT_MIN)
    thr_valid = jnp.maximum(thr, INT_MIN + 1)
    has_tie = jnp.max(jnp.where(tie, 1.0, 0.0)) > 0.0

    def write_fast():
        write_mask(ge(thr_valid))

    def write_ties():
        n_gt = count(lambda bc: (lambda k, col, tb=bc(thr): k > tb))
        need = topk - n_gt

        def eq_below(cand):
            return count(lambda bc: (lambda k, col, tb=bc(thr), cb=bc(cand): jnp.logical_and(k == tb, col < cb)))
        cut = _tie_index_cut(eq_below, need, tq, nbits)
        cut = jnp.where(tie, cut, jnp.int32(2 ** 30))
        write_mask(lambda bc: (lambda k, col, tv=bc(thr_valid), tb=bc(thr), cb=bc(cut): jnp.logical_and(
            k >= tv, jnp.logical_or(k > tb, col <= cb))))

    lax.cond(has_tie, write_ties, write_fast)

    def fill(c, carry):
        mask_ref[0, c] = jnp.full((tq, kc), NEG, BF16)
        return carry
    lax.fori_loop(nch, n_chunks_total, fill, 0)


def _prompt_select(qi, wi, ki, *, n_idx, name):
    s = qi.shape[0]
    tq = kc = _pick(s, (256, 128))
    nq = s // tq
    topk = min(TOPK_MAX, s // 4)
    nbits = max(1, int(np.ceil(np.log2(s))))
    return pl.pallas_call(
        functools.partial(_idx_kernel, tq=tq, kc=kc, n_idx=n_idx, n_chunks_total=nq, topk=topk, nbits=nbits),
        grid=(nq,),
        in_specs=[pl.BlockSpec((tq, n_idx * LANES), lambda i: (i, 0)),
                  pl.BlockSpec((tq, LANES), lambda i: (i, 0)),
                  pl.BlockSpec((s, LANES), lambda i: (0, 0))],
        out_specs=pl.BlockSpec((1, nq, tq, kc), lambda i: (i, 0, 0, 0)),
        out_shape=jax.ShapeDtypeStruct((nq, nq, tq, kc), BF16),
        scratch_shapes=[pltpu.VMEM((nq, tq, kc), I32), pltpu.VMEM((n_idx, tq, LANES), F32)],
        compiler_params=_cparams(("parallel",)),
        name=name,
    )(qi, wi, ki)


def _attn_kernel(q_ref, k_ref, v_ref, mask_ref, o_ref, m_scr, l_scr, acc_scr, *, rq, rk, kc, hps, scale2):
    i = pl.program_id(1)
    m_scr[...] = jnp.full(m_scr.shape, NEG, F32)
    l_scr[...] = jnp.zeros(l_scr.shape, F32)
    acc_scr[...] = jnp.zeros(acc_scr.shape, F32)

    def body(c, carry):
        bias = jnp.concatenate(
            [jnp.concatenate([mask_ref[a, c * rk + b] for b in range(rk)], axis=1) for a in range(rq)],
            axis=0).astype(F32)
        rows = pl.ds(pl.multiple_of(c * kc, kc), kc)
        for hh in range(hps):
            sl = slice(hh * LANES, (hh + 1) * LANES)
            s = lax.dot_general(q_ref[:, sl], k_ref[rows, sl], _NT, preferred_element_type=F32) * scale2 + bias
            m = m_scr[hh]
            m_new = jnp.maximum(m, jnp.max(s, axis=-1, keepdims=True))
            alpha = jnp.exp2(m - m_new)
            p = jnp.exp2(s - m_new)
            l_scr[hh] = alpha * l_scr[hh] + jnp.sum(p, axis=-1, keepdims=True)
            acc_scr[hh] = alpha * acc_scr[hh] + jnp.dot(p.astype(BF16), v_ref[rows, sl],
                                                        preferred_element_type=F32)
            m_scr[hh] = m_new
        return carry

    lax.fori_loop(0, i + 1, body, 0)
    for hh in range(hps):
        o_ref[:, hh * LANES:(hh + 1) * LANES] = (acc_scr[hh] / l_scr[hh]).astype(o_ref.dtype)


def _prompt_attention(q, k, v, mask, *, name):
    s, d_b = q.shape
    nqs, nks, tqs, kcs = mask.shape
    tq = kc = _pick(s, (512, 256, 128))
    tq, kc = max(tq, tqs), max(kc, kcs)
    rq, rk = tq // tqs, kc // kcs
    hps = 2 if (d_b // LANES) % 2 == 0 else 1
    wblk = hps * LANES
    return pl.pallas_call(
        functools.partial(_attn_kernel, rq=rq, rk=rk, kc=kc, hps=hps,
                          scale2=float(LANES ** -0.5 * np.log2(np.e))),
        grid=(d_b // wblk, s // tq),
        in_specs=[pl.BlockSpec((tq, wblk), lambda h, i: (i, h)),
                  pl.BlockSpec((s, wblk), lambda h, i: (0, h)),
                  pl.BlockSpec((s, wblk), lambda h, i: (0, h)),
                  pl.BlockSpec((rq, nks, tqs, kcs), lambda h, i: (i, 0, 0, 0))],
        out_specs=pl.BlockSpec((tq, wblk), lambda h, i: (i, h)),
        out_shape=jax.ShapeDtypeStruct((s, d_b), BF16),
        scratch_shapes=[pltpu.VMEM((hps, tq, 1), F32), pltpu.VMEM((hps, tq, 1), F32),
                        pltpu.VMEM((hps, tq, LANES), F32)],
        compiler_params=_cparams(("parallel", "arbitrary")),
        name=name,
    )(q, k, v, mask)


def _sample_idx_kernel(pt_ref, *refs, g_pages, n_steps, n_idx, dec_seq, topk, nbits, past):
    del pt_ref
    pages = refs[:g_pages]
    qi_ref, wcol_ref, kinew_ref, bias_past_ref, bias_new_ref, kb_scr, key_scr, keyn_scr = refs[g_pages:]
    j = pl.program_id(1)
    pg = g_pages * PAGE_SIZE
    rows = 8
    row_id = lax.broadcasted_iota(I32, (rows, 1), 0)

    def scores_of(kmat):
        d = lax.dot_general(qi_ref[0], kmat, _NT, preferred_element_type=F32)
        wr = jnp.maximum(d, 0.0) * wcol_ref[0]
        s4 = jnp.sum(wr.reshape(dec_seq, n_idx, kmat.shape[0]), axis=1)
        return jnp.concatenate([s4, jnp.zeros((rows - dec_seq, kmat.shape[0]), F32)], axis=0)

    for r in range(g_pages):
        kb_scr[r * PAGE_SIZE:(r + 1) * PAGE_SIZE, :] = pages[r][0].astype(BF16)
    key_scr[j] = jnp.where(row_id < dec_seq, _sortable_key(scores_of(kb_scr[...])), INT_MIN)

    @pl.when(j == n_steps - 1)
    def _():
        sn = scores_of(kinew_ref[0])
        coln = lax.broadcasted_iota(I32, (1, LANES), 1)
        okn = jnp.logical_and(jnp.logical_and(coln < dec_seq, coln <= row_id), row_id < dec_seq)
        keyn_scr[...] = jnp.where(okn, _sortable_key(sn), INT_MIN)

        def count(pred):
            cnt = jnp.zeros((rows, LANES), F32)
            for c in range(n_steps):
                p = pred(key_scr[c], c * pg + lax.broadcasted_iota(I32, (1, pg), 1))
                for jj in range(pg // LANES):
                    cnt = cnt + jnp.where(p[:, jj * LANES:(jj + 1) * LANES], 1.0, 0.0)
            cnt = cnt + jnp.where(pred(keyn_scr[...], past + coln), 1.0, 0.0)
            return jnp.sum(cnt, axis=-1, keepdims=True)

        thr, n_ge = _kth_largest_key(lambda cand: count(lambda k, col: k >= cand), rows, topk)
        n_gt = count(lambda k, col: k > thr)
        tie = jnp.logical_and(n_ge > topk, thr > INT_MIN)
        thr_valid = jnp.maximum(thr, INT_MIN + 1)
        need = topk - n_gt
        cut = _tie_index_cut(
            lambda cand: count(lambda k, col: jnp.logical_and(k == thr, col < cand)), need, rows, nbits)
        cut = jnp.where(tie, cut, jnp.int32(2 ** 30))

        def sel(k, col):
            return jnp.logical_and(k >= thr_valid, jnp.logical_or(k > thr, col <= cut))

        half = pg // 2
        for c in range(n_steps):
            b = jnp.where(sel(key_scr[c], c * pg + lax.broadcasted_iota(I32, (1, pg), 1)), 0.0, NEG)
            bias_past_ref[0, 2 * c] = b[:, :half]
            bias_past_ref[0, 2 * c + 1] = b[:, half:]
        bias_new_ref[0] = jnp.where(sel(keyn_scr[...], past + coln), 0.0, NEG)


def _sample_select(page_table, cache_kidx, qi_rows, wcol, kinew, *, n_idx, dec_seq, name):
    bd, n_pages = page_table.shape
    g_pages = _pick(n_pages, (8, 4, 2, 1))
    n_steps = n_pages // g_pages
    pg = g_pages * PAGE_SIZE
    past = n_pages * PAGE_SIZE
    topk = min(TOPK_MAX, (past + dec_seq) // 4)
    nbits = max(1, int(np.ceil(np.log2(past + LANES))))
    rq = dec_seq * n_idx

    def page_spec(r):
        return pl.BlockSpec((1, PAGE_SIZE, LANES), lambda b, j, pt: (pt[b, j * g_pages + r], 0, 0))

    grid_spec = pltpu.PrefetchScalarGridSpec(
        num_scalar_prefetch=1,
        grid=(bd, n_steps),
        in_specs=[page_spec(r) for r in range(g_pages)] + [
            pl.BlockSpec((1, rq, LANES), lambda b, j, pt: (b, 0, 0)),
            pl.BlockSpec((1, rq, 1), lambda b, j, pt: (b, 0, 0)),
            pl.BlockSpec((1, LANES, LANES), lambda b, j, pt: (b, 0, 0))],
        out_specs=[pl.BlockSpec((1, 2 * n_steps, 8, pg // 2), lambda b, j, pt: (b, 0, 0, 0)),
                   pl.BlockSpec((1, 8, LANES), lambda b, j, pt: (b, 0, 0))],
        scratch_shapes=[pltpu.VMEM((pg, LANES), BF16), pltpu.VMEM((n_steps, 8, pg), I32),
                        pltpu.VMEM((8, LANES), I32)],
    )
    return pl.pallas_call(
        functools.partial(_sample_idx_kernel, g_pages=g_pages, n_steps=n_steps, n_idx=n_idx,
                          dec_seq=dec_seq, topk=topk, nbits=nbits, past=past),
        grid_spec=grid_spec,
        out_shape=[jax.ShapeDtypeStruct((bd, 2 * n_steps, 8, pg // 2), F32),
                   jax.ShapeDtypeStruct((bd, 8, LANES), F32)],
        compiler_params=_cparams(("parallel", "arbitrary")),
        name=name,
    )(page_table, *([cache_kidx] * g_pages), qi_rows, wcol, kinew)


def _sample_attn_kernel(pt_ref, *refs, g_pages, n_steps, n_heads, scale):
    del pt_ref
    kpages = refs[:g_pages]
    vpages = refs[g_pages:2 * g_pages]
    (qblk_ref, bias_ref, biasn_ref, knew_ref, vnew_ref, o_ref,
     kb_scr, vb_scr, m_scr, l_scr, acc_scr) = refs[2 * g_pages:]
    j = pl.program_id(1)
    rows = 8 * n_heads

    @pl.when(j == 0)
    def _():
        m_scr[...] = jnp.full(m_scr.shape, NEG, F32)
        l_scr[...] = jnp.zeros(l_scr.shape, F32)
        acc_scr[...] = jnp.zeros(acc_scr.shape, F32)

    def update(kb, vb, bias8):
        st = jnp.dot(kb, qblk_ref[0], preferred_element_type=F32)
        s = st.T * scale + jnp.concatenate([bias8] * n_heads, axis=0)
        m = m_scr[...]
        m_new = jnp.maximum(m, jnp.max(s, axis=-1, keepdims=True))
        alpha = jnp.exp2(m - m_new)
        p = jnp.exp2(s - m_new)
        l_scr[...] = alpha * l_scr[...] + jnp.sum(p, axis=-1, keepdims=True)
        acc_scr[...] = alpha * acc_scr[...] + jnp.dot(p.astype(BF16), vb, preferred_element_type=F32)
        m_scr[...] = m_new

    for r in range(g_pages):
        rs = slice(r * PAGE_SIZE, (r + 1) * PAGE_SIZE)
        for h in range(n_heads):
            cs = slice(h * LANES, (h + 1) * LANES)
            head_rows = pl.ds(h, PAGE_SIZE, stride=n_heads)
            kb_scr[rs, cs] = kpages[r][0, 0, head_rows, :].astype(BF16)
            vb_scr[rs, cs] = vpages[r][0, 0, head_rows, :].astype(BF16)
    update(kb_scr[...], vb_scr[...], bias_ref[0, 0])

    @pl.when(j == n_steps - 1)
    def _():
        update(knew_ref[0], vnew_ref[0], biasn_ref[0])
        for h in range(n_heads):
            rs = slice(h * 8, (h + 1) * 8)
            cs = slice(h * LANES, (h + 1) * LANES)
            o_ref[0, :, cs] = acc_scr[rs, cs] / l_scr[rs, :]


def _sample_attention(page_table, cache_k, cache_v, layer, qblk, bias_past, bias_new, knew, vnew, *, n_heads, name):
    bd, n_pages = page_table.shape
    d_b = n_heads * LANES
    n_steps = bias_past.shape[1]
    g_pages = n_pages // n_steps
    pg = g_pages * PAGE_SIZE
    rows = 8 * n_heads

    def page_spec(r):
        return pl.BlockSpec((1, 1, PAGE_SIZE * n_heads, LANES),
                            lambda b, j, pt: (layer, pt[b, j * g_pages + r], 0, 0))

    grid_spec = pltpu.PrefetchScalarGridSpec(
        num_scalar_prefetch=1,
        grid=(bd, n_steps),
        in_specs=[page_spec(r) for r in range(g_pages)] * 2 + [
            pl.BlockSpec((1, d_b, rows), lambda b, j, pt: (b, 0, 0)),
            pl.BlockSpec((1, 1, 8, pg), lambda b, j, pt: (b, j, 0, 0)),
            pl.BlockSpec((1, 8, LANES), lambda b, j, pt: (b, 0, 0)),
            pl.BlockSpec((1, LANES, d_b), lambda b, j, pt: (b, 0, 0)),
            pl.BlockSpec((1, LANES, d_b), lambda b, j, pt: (b, 0, 0))],
        out_specs=pl.BlockSpec((1, 8, d_b), lambda b, j, pt: (b, 0, 0)),
        scratch_shapes=[pltpu.VMEM((pg, d_b), BF16), pltpu.VMEM((pg, d_b), BF16),
                        pltpu.VMEM((rows, 1), F32), pltpu.VMEM((rows, 1), F32), pltpu.VMEM((rows, d_b), F32)],
    )
    return pl.pallas_call(
        functools.partial(_sample_attn_kernel, g_pages=g_pages, n_steps=n_steps, n_heads=n_heads,
                          scale=float(LANES ** -0.5 * np.log2(np.e))),
        grid_spec=grid_spec,
        out_shape=jax.ShapeDtypeStruct((bd, 8, d_b), F32),
        compiler_params=_cparams(("parallel", "arbitrary")),
        name=name,
    )(page_table, *([cache_k] * g_pages), *([cache_v] * g_pages), qblk, bias_past, bias_new, knew, vnew)


def _merge_kernel(h_ref, a_ref, attn_ref, zs_ref, wga_ref, wgb_ref, wua_ref, wub_ref, o_ref, bg_scr):
    @pl.when(pl.program_id(1) == 0)
    def _():
        bg_scr[...] = (attn_ref[...].astype(F32) * zs_ref[...].astype(F32)).astype(BF16)

    h = h_ref[...]
    ga = jnp.dot(h, wga_ref[...], preferred_element_type=F32)
    gb = jnp.dot(h, wgb_ref[...], preferred_element_type=F32)
    a_up = jnp.dot(a_ref[...], wua_ref[...], preferred_element_type=F32)
    b_up = jnp.dot(bg_scr[...], wub_ref[...], preferred_element_type=F32)
    o_ref[...] = (_sigmoid(ga) * a_up + _sigmoid(gb) * b_up).astype(o_ref.dtype)


def _merge(h, a_gated, attn, zs, w_ga, w_gb, w_ua, w_ub, *, name):
    m, d = h.shape
    d_a, d_b = a_gated.shape[1], attn.shape[1]
    tm = _pick(m, (512, 256, 128))
    tn = _pick(d, (256, 128))
    row = lambda i, j: (i, 0)
    col = lambda i, j: (0, j)
    return pl.pallas_call(
        _merge_kernel,
        grid=(m // tm, d // tn),
        in_specs=[pl.BlockSpec((tm, d), row), pl.BlockSpec((tm, d_a), row),
                  pl.BlockSpec((tm, d_b), row), pl.BlockSpec((tm, d_b), row),
                  pl.BlockSpec((d, tn), col), pl.BlockSpec((d, tn), col),
                  pl.BlockSpec((d_a, tn), col), pl.BlockSpec((d_b, tn), col)],
        out_specs=pl.BlockSpec((tm, tn), lambda i, j: (i, j)),
        out_shape=jax.ShapeDtypeStruct((m, d), BF16),
        scratch_shapes=[pltpu.VMEM((tm, d_b), BF16)],
        compiler_params=_cparams(("parallel", "arbitrary")),
        name=name,
    )(h, a_gated, attn, zs, w_ga, w_gb, w_ua, w_ub)


def _resid_kernel(mg_ref, w_ref, x_ref, o_ref):
    o_ref[...] = x_ref[...] + jnp.dot(mg_ref[...], w_ref[...], preferred_element_type=F32)


def _resid_proj(merged, w_o, x, *, name):
    m, d = x.shape
    tm = _pick(m, (1024, 512, 256, 128))
    tn = _pick(d, (512, 256, 128))
    return pl.pallas_call(
        _resid_kernel,
        grid=(m // tm, d // tn),
        in_specs=[pl.BlockSpec((tm, merged.shape[1]), lambda i, j: (i, 0)),
                  pl.BlockSpec((merged.shape[1], tn), lambda i, j: (0, j)),
                  pl.BlockSpec((tm, tn), lambda i, j: (i, j))],
        out_specs=pl.BlockSpec((tm, tn), lambda i, j: (i, j)),
        out_shape=jax.ShapeDtypeStruct((m, d), F32),
        compiler_params=_cparams(("parallel", "arbitrary")),
        name=name,
    )(merged, w_o, x)


def _ple_kernel(hn_ref, wg_ref, p_ref, wp_ref, x1_ref, o_ref):
    gate = _sigmoid(jnp.dot(hn_ref[...], wg_ref[...], preferred_element_type=F32))
    emb = jnp.dot(p_ref[...], wp_ref[...], preferred_element_type=F32)
    o_ref[...] = x1_ref[...] + gate * emb


def _ple(hn, w_gate, p, w_proj, x1, *, name):
    m, d = x1.shape
    tm = _pick(m, (1024, 512, 256, 128))
    tn = _pick(d, (512, 256, 128))
    return pl.pallas_call(
        _ple_kernel,
        grid=(m // tm, d // tn),
        in_specs=[pl.BlockSpec((tm, d), lambda i, j: (i, 0)),
                  pl.BlockSpec((d, tn), lambda i, j: (0, j)),
                  pl.BlockSpec((tm, p.shape[1]), lambda i, j: (i, 0)),
                  pl.BlockSpec((p.shape[1], tn), lambda i, j: (0, j)),
                  pl.BlockSpec((tm, tn), lambda i, j: (i, j))],
        out_specs=pl.BlockSpec((tm, tn), lambda i, j: (i, j)),
        out_shape=jax.ShapeDtypeStruct((m, d), F32),
        compiler_params=_cparams(("parallel", "arbitrary")),
        name=name,
    )(hn, w_gate, p, w_proj, x1)


def _rope_tables(pos):
    half = LANES // 2
    inv = ROPE_THETA ** (-jnp.arange(half, dtype=F32) / half)
    ang = pos.astype(F32)[:, None] * inv[None, :]
    cos, sin = jnp.cos(ang), jnp.sin(ang)
    return jnp.concatenate([cos, cos], axis=-1), jnp.concatenate([-sin, sin], axis=-1)


def _split_weights(w_in, d_model, d_a, d_b, n_idx):
    widths = [3 * d_a, d_b, d_b, d_b, d_b, n_idx * LANES, LANES + n_idx, d_model, d_model]
    offs = np.concatenate([[0], np.cumsum(widths)])
    parts = [w_in[:, int(offs[i]):int(offs[i + 1])].astype(BF16) for i in range(len(widths))]
    parts[6] = jnp.pad(parts[6], ((0, 0), (0, 2 * LANES - parts[6].shape[1])))
    return parts


def _layer(x, p, pos, tag, w, ws, bs, *, n_heads, n_idx, emit_vn):
    m, d_model = x.shape
    cosf, sinf = _rope_tables(pos)
    h = _rmsnorm(x, w["g_norm"], f"{tag}_norm")
    a_out = _branch_a(h, w["w_a"], w["g_va"], w["b_va"], ws, bs, emit_vn=emit_vn, name=f"{tag}_branch_a")
    gq = jnp.tile(w["g_q"], n_heads).reshape(1, -1)
    gk = jnp.tile(w["g_k"], n_heads).reshape(1, -1)
    (q_bf,) = _proj(h, w["w_q"], kind="norm_rope", n_f32=0, n_bf16=1, gain=gq, cos=cosf, sin=sinf, name=f"{tag}_q")
    k_f32, k_bf = _proj(h, w["w_k"], kind="norm_rope", n_f32=1, n_bf16=1, gain=gk, cos=cosf, sin=sinf,
                        name=f"{tag}_k")
    v_f32, v_bf = _proj(h, w["w_v"], kind="plain", n_f32=1, n_bf16=1, name=f"{tag}_v")
    (zs,) = _proj(h, w["w_zb"], kind="silu", n_f32=0, n_bf16=1, name=f"{tag}_zb")
    (qi,) = _proj(h, w["w_qi"], kind="rope", n_f32=0, n_bf16=1, cos=cosf, sin=sinf, name=f"{tag}_qi")
    ki_f32, ki_bf, wi = _proj(h, w["w_kiwi"], kind="kiwi", n_f32=0, n_bf16=0, cos=cosf, sin=sinf,
                              scale=float(n_idx ** -0.5 * LANES ** -0.5), name=f"{tag}_kiwi")
    return dict(h=h, a=a_out, q=q_bf, k=k_f32, kb=k_bf, v=v_f32, vb=v_bf, zs=zs, qi=qi, ki=ki_f32, kib=ki_bf, wi=wi)


def _finish(x, p, t, attn, tag, w):
    merged = _merge(t["h"], t["a"][0], attn, t["zs"], w["w_ga"], w["w_gb"], w["w_ua"], w["w_ub"], name=f"{tag}_merge")
    x1 = _resid_proj(merged, w["w_o"], x, name=f"{tag}_resid")
    hn = _rmsnorm(x1, w["g_ple"], f"{tag}_ple_norm")
    return _ple(hn, w["w_pg"], p.astype(BF16), w["w_pp"], x1, name=f"{tag}_ple")


def kernel(x_prompt, x_sample, p_prompt, p_sample, cache_k, cache_v, cache_kidx, page_table, g_norm, w_in, g_va,
           b_va, w_s, b_s, g_q, g_k, w_up_a, w_up_b, w_o, g_ple, w_ple_gate, w_ple_proj):
    depth = w_in.shape[0]
    batch, seq, d_model = x_prompt.shape
    bd, dec_seq, _ = x_sample.shape
    n_phys, _, n_heads, head_dim = cache_k.shape[1:]
    n_groups = w_s.shape[1]
    d_a, d_b = w_up_a.shape[1], w_up_b.shape[1]
    n_idx = w_in.shape[2] - (3 * d_a + 4 * d_b + LANES + 2 * d_model)
    n_idx = n_idx // (LANES + 1)
    n_pages = page_table.shape[1]
    past = n_pages * PAGE_SIZE
    assert batch == 1 and head_dim == LANES and cache_kidx.shape[-1] == LANES and d_a == n_groups * LANES
    assert seq % CHUNK == 0 and bd * dec_seq == CHUNK and dec_seq <= 8 and d_b == n_heads * LANES
    ms = bd * dec_seq

    yp = x_prompt.reshape(seq, d_model)
    ys = x_sample.reshape(ms, d_model)
    pos_p = jnp.arange(seq)
    pos_s = jnp.tile(past + jnp.arange(dec_seq), bd)
    outs = {k: [] for k in ("kp", "vp", "kip", "ks", "vs", "kis", "vas")}
    tril = jnp.tril(jnp.ones((CHUNK, CHUNK), bool))
    cache_k2 = cache_k.reshape(depth, n_phys, PAGE_SIZE * n_heads, LANES)
    cache_v2 = cache_v.reshape(depth, n_phys, PAGE_SIZE * n_heads, LANES)
    for i in range(depth):
        parts = _split_weights(w_in[i], d_model, d_a, d_b, n_idx)
        w = dict(zip(("w_a", "w_q", "w_k", "w_v", "w_zb", "w_qi", "w_kiwi", "w_ga", "w_gb"), parts))
        w.update(g_norm=g_norm[i], g_va=g_va[i], b_va=b_va[i], g_q=g_q[i], g_k=g_k[i], g_ple=g_ple[i],
                 w_ua=w_up_a[i].astype(BF16), w_ub=w_up_b[i].astype(BF16), w_o=w_o[i].astype(BF16),
                 w_pg=w_ple_gate[i].astype(BF16), w_pp=w_ple_proj[i].astype(BF16))
        ws_p = jnp.where(tril[None], w_s[i], 0.0).astype(BF16)
        bs_p = jnp.broadcast_to(b_s[i][:, :, None], (n_groups, CHUNK, LANES))
        small = jnp.where(tril[None, :dec_seq, :dec_seq], w_s[i][:, :dec_seq, :dec_seq], 0.0)
        ws_s = jnp.einsum("ab,gts->gatbs", jnp.eye(bd, dtype=F32), small).reshape(n_groups, ms, ms).astype(BF16)
        bs_s = jnp.broadcast_to(jnp.tile(b_s[i][:, :dec_seq], (1, bd))[:, :, None], (n_groups, ms, LANES))

        tp = _layer(yp, p_prompt[i, 0], pos_p, "prompt", w, ws_p, bs_p, n_heads=n_heads, n_idx=n_idx, emit_vn=False)
        mask = _prompt_select(tp["qi"], tp["wi"], tp["kib"], n_idx=n_idx, name="prompt_select")
        attn_p = _prompt_attention(tp["q"], tp["kb"], tp["vb"], mask, name="prompt_attention")
        yp_new = _finish(yp, p_prompt[i, 0], tp, attn_p, "prompt", w)

        tsm = _layer(ys, p_sample[i].reshape(ms, -1), pos_s, "sample", w, ws_s, bs_s, n_heads=n_heads, n_idx=n_idx,
                     emit_vn=True)
        qi_rows = tsm["qi"].reshape(bd, dec_seq * n_idx, LANES)
        wcol = tsm["wi"][:, :n_idx].reshape(bd, dec_seq * n_idx, 1)
        kinew = jnp.pad(tsm["kib"].reshape(bd, dec_seq, LANES), ((0, 0), (0, LANES - dec_seq), (0, 0)))
        bias_past, bias_new = _sample_select(page_table, cache_kidx[i], qi_rows, wcol, kinew, n_idx=n_idx,
                                             dec_seq=dec_seq, name="sample_select")
        q4 = jnp.pad(tsm["q"].reshape(bd, dec_seq, n_heads, LANES), ((0, 0), (0, 8 - dec_seq), (0, 0), (0, 0)))
        qblk = jnp.einsum("bthd,hg->bhdgt", q4, jnp.eye(n_heads, dtype=BF16)).reshape(bd, d_b, n_heads * 8)
        padrows = ((0, 0), (0, LANES - dec_seq), (0, 0))
        knew = jnp.pad(tsm["kb"].reshape(bd, dec_seq, d_b), padrows)
        vnew = jnp.pad(tsm["vb"].reshape(bd, dec_seq, d_b), padrows)
        attn_s8 = _sample_attention(page_table, cache_k2, cache_v2, i, qblk, bias_past, bias_new,
                                    knew, vnew, n_heads=n_heads, name="sample_attention")
        attn_s = attn_s8[:, :dec_seq].reshape(ms, d_b).astype(BF16)
        ys_new = _finish(ys, p_sample[i].reshape(ms, -1), tsm, attn_s, "sample", w)

        outs["kp"].append(tp["k"].reshape(batch, seq, n_heads, LANES))
        outs["vp"].append(tp["v"].reshape(batch, seq, n_heads, LANES))
        outs["kip"].append(tp["ki"].reshape(batch, seq, LANES))
        outs["ks"].append(tsm["k"].reshape(bd, dec_seq, n_heads, LANES))
        outs["vs"].append(tsm["v"].reshape(bd, dec_seq, n_heads, LANES))
        outs["kis"].append(tsm["ki"].reshape(bd, dec_seq, LANES))
        outs["vas"].append(tsm["a"][1].reshape(bd, dec_seq, d_a))
        yp, ys = yp_new, ys_new

    return (yp.reshape(batch, seq, d_model), ys.reshape(bd, dec_seq, d_model),
            jnp.stack(outs["kp"]), jnp.stack(outs["vp"]), jnp.stack(outs["kip"]),
            jnp.stack(outs["ks"]), jnp.stack(outs["vs"]), jnp.stack(outs["kis"]), jnp.stack(outs["vas"]))
```

```python
import functools

import jax
import jax.numpy as jnp
import numpy as np
from jax import lax
from jax.experimental import pallas as pl
from jax.experimental.pallas import tpu as pltpu

F32 = jnp.float32
BF16 = jnp.bfloat16
I32 = jnp.int32

LANES = 128
PAGE_SIZE = 128
CHUNK = 128
TOPK_MAX = 256
ROPE_THETA = 10000.0
EPS = 1e-6
NEG = -1e30
INT_MIN = -(2 ** 31)
VMEM_LIMIT = 56 * 1024 * 1024

_NT = (((1,), (1,)), ((), ()))


def _cparams(sem):
    return pltpu.CompilerParams(dimension_semantics=sem, vmem_limit_bytes=VMEM_LIMIT)


def _sigmoid(x):
    return 1.0 / (1.0 + jnp.exp(-x))


def _pick(n, cands):
    for c in cands:
        if n % c == 0:
            return c
    return n


def _rmsnorm_kernel(x_ref, g_ref, o_ref):
    x = x_ref[...]
    ms = jnp.mean(x * x, axis=-1, keepdims=True)
    o_ref[...] = (x * lax.rsqrt(ms + EPS) * g_ref[...]).astype(o_ref.dtype)


def _rmsnorm(x, g, name):
    m, d = x.shape
    tm = _pick(m, (256, 128))
    return pl.pallas_call(
        _rmsnorm_kernel,
        grid=(m // tm,),
        in_specs=[pl.BlockSpec((tm, d), lambda i: (i, 0)),
                  pl.BlockSpec((1, d), lambda i: (0, 0))],
        out_specs=pl.BlockSpec((tm, d), lambda i: (i, 0)),
        out_shape=jax.ShapeDtypeStruct((m, d), BF16),
        compiler_params=_cparams(("parallel",)),
        name=name,
    )(x, g.reshape(1, d))


def _rope(x, cos, sin_signed):
    return x * cos + pltpu.roll(x, LANES // 2, axis=1) * sin_signed


def _proj_kernel(*refs, kind, n_f32, n_bf16, tn, scale, t_chunk):
    h_ref, w_ref = refs[0], refs[1]
    acc = jnp.dot(h_ref[...], w_ref[...], preferred_element_type=F32)
    if t_chunk:
        t_ref, refs = refs[-1], refs[:-1]
        for a in range(acc.shape[0] // t_chunk):
            t_ref[a] = acc[a * t_chunk:(a + 1) * t_chunk, :].T.astype(BF16)
    pos = 2
    if kind == "norm_rope":
        g_ref, cos_ref, sin_ref = refs[pos:pos + 3]
        pos += 3
    elif kind in ("rope", "kiwi"):
        cos_ref, sin_ref = refs[pos:pos + 2]
        pos += 2
    outs = refs[pos:]
    if kind == "kiwi":
        ki = _rope(acc[:, :LANES], cos_ref[...], sin_ref[...])
        outs[0][...] = ki
        outs[1][...] = ki.astype(BF16)
        outs[2][...] = acc[:, LANES:] * scale
        return
    for j in range(tn // LANES):
        sl = slice(j * LANES, (j + 1) * LANES)
        x = acc[:, sl]
        if kind == "norm_rope":
            ms = jnp.mean(x * x, axis=-1, keepdims=True)
            x = x * lax.rsqrt(ms + EPS) * g_ref[:, sl]
            x = _rope(x, cos_ref[...], sin_ref[...])
        elif kind == "rope":
            x = _rope(x, cos_ref[...], sin_ref[...])
        elif kind == "silu":
            x = x * _sigmoid(x)
        k = 0
        for _ in range(n_f32):
            outs[k][:, sl] = x
            k += 1
        for _ in range(n_bf16):
            outs[k][:, sl] = x.astype(BF16)
            k += 1


def _proj(h, w, *, kind, n_f32, n_bf16, name, gain=None, cos=None, sin=None, scale=1.0, t_chunk=0):
    m, kdim = h.shape
    n = w.shape[1]
    tm = _pick(m, (1024, 512, 256, 128))
    assert t_chunk == 0 or tm % t_chunk == 0
    tn = 256 if kind == "kiwi" else _pick(n, (512, 256, 128))
    in_specs = [pl.BlockSpec((tm, kdim), lambda i, j: (i, 0)),
                pl.BlockSpec((kdim, tn), lambda i, j: (0, j))]
    args = [h, w]
    if kind == "norm_rope":
        in_specs.append(pl.BlockSpec((1, tn), lambda i, j: (0, j)))
        args.append(gain)
    if kind in ("norm_rope", "rope", "kiwi"):
        in_specs += [pl.BlockSpec((tm, LANES), lambda i, j: (i, 0))] * 2
        args += [cos, sin]
    if kind == "kiwi":
        out_shape = [jax.ShapeDtypeStruct((m, LANES), F32), jax.ShapeDtypeStruct((m, LANES), BF16),
                     jax.ShapeDtypeStruct((m, LANES), F32)]
        out_specs = [pl.BlockSpec((tm, LANES), lambda i, j: (i, 0))] * 3
    else:
        out_shape = ([jax.ShapeDtypeStruct((m, n), F32)] * n_f32 + [jax.ShapeDtypeStruct((m, n), BF16)] * n_bf16)
        out_specs = [pl.BlockSpec((tm, tn), lambda i, j: (i, j))] * (n_f32 + n_bf16)
    if t_chunk:
        out_shape = out_shape + [jax.ShapeDtypeStruct((m // t_chunk, n, t_chunk), BF16)]
        out_specs = out_specs + [pl.BlockSpec((tm // t_chunk, tn, t_chunk), lambda i, j: (i, j, 0))]
    return pl.pallas_call(
        functools.partial(_proj_kernel, kind=kind, n_f32=n_f32, n_bf16=n_bf16, tn=tn, scale=scale,
                          t_chunk=t_chunk),
        grid=(m // tm, n // tn),
        in_specs=in_specs,
        out_specs=out_specs,
        out_shape=out_shape,
        compiler_params=_cparams(("parallel", "arbitrary")),
        name=name,
    )(*args)


def _branch_a_kernel(h_ref, w_ref, gva_ref, bva_ref, ws_ref, bs_ref, *rest, tm, tn, d_a, emit_vn):
    if emit_vn:
        a_ref, vn_ref, scr = rest
    else:
        a_ref, scr = rest
    n = pl.program_id(1)
    nseg = d_a // tn
    per = tn // LANES
    scr[n] = jnp.dot(h_ref[...], w_ref[...], preferred_element_type=F32)

    @pl.when(n == 3 * nseg - 1)
    def _():
        s1 = jnp.zeros((tm, 1), F32)
        for j in range(nseg):
            s1 = s1 + jnp.sum(scr[nseg + j], axis=-1, keepdims=True)
        mu = s1 / d_a
        s2 = jnp.zeros((tm, 1), F32)
        for j in range(nseg):
            dlt = scr[nseg + j] - mu
            s2 = s2 + jnp.sum(dlt * dlt, axis=-1, keepdims=True)
        rstd = lax.rsqrt(s2 / d_a + EPS)
        for g in range(d_a // LANES):
            t_idx, sl = g // per, slice((g % per) * LANES, (g % per + 1) * LANES)
            gsl = slice(g * LANES, (g + 1) * LANES)
            vn = (scr[nseg + t_idx][:, sl] - mu) * rstd * gva_ref[:, gsl] + bva_ref[:, gsl]
            if emit_vn:
                vn_ref[:, gsl] = vn
            vb = vn.astype(BF16)
            u = scr[t_idx][:, sl]
            z = scr[2 * nseg + t_idx][:, sl]
            for c in range(tm // CHUNK):
                rs = slice(c * CHUNK, (c + 1) * CHUNK)
                mixed = jnp.dot(ws_ref[g], vb[rs], preferred_element_type=F32) + bs_ref[g]
                a_ref[rs, gsl] = (u[rs] * mixed * (z[rs] * _sigmoid(z[rs]))).astype(BF16)


def _branch_a(h, w_a, g_va, b_va, ws, bs, *, emit_vn, name):
    m, kdim = h.shape
    d_a = w_a.shape[1] // 3
    tm = _pick(m, (512, 256, 128))
    tn = _pick(d_a, (512, 256, 128))
    ng = d_a // LANES
    out_shape = [jax.ShapeDtypeStruct((m, d_a), BF16)]
    out_specs = [pl.BlockSpec((tm, d_a), lambda i, j: (i, 0))]
    if emit_vn:
        out_shape.append(jax.ShapeDtypeStruct((m, d_a), F32))
        out_specs.append(pl.BlockSpec((tm, d_a), lambda i, j: (i, 0)))
    return pl.pallas_call(
        functools.partial(_branch_a_kernel, tm=tm, tn=tn, d_a=d_a, emit_vn=emit_vn),
        grid=(m // tm, 3 * d_a // tn),
        in_specs=[pl.BlockSpec((tm, kdim), lambda i, j: (i, 0)),
                  pl.BlockSpec((kdim, tn), lambda i, j: (0, j)),
                  pl.BlockSpec((1, d_a), lambda i, j: (0, 0)),
                  pl.BlockSpec((1, d_a), lambda i, j: (0, 0)),
                  pl.BlockSpec((ng, CHUNK, CHUNK), lambda i, j: (0, 0, 0)),
                  pl.BlockSpec((ng, CHUNK, LANES), lambda i, j: (0, 0, 0))],
        out_specs=out_specs,
        out_shape=out_shape,
        scratch_shapes=[pltpu.VMEM((3 * d_a // tn, tm, tn), F32)],
        compiler_params=_cparams(("parallel", "arbitrary")),
        name=name,
    )(h, w_a, g_va.reshape(1, d_a), b_va.reshape(1, d_a), ws, bs)


def _sortable_key(score):
    bits = lax.bitcast_convert_type(score, I32)
    return jnp.where(bits < 0, (bits ^ 0x7FFFFFFF) + 1, bits)


def _kth_largest_key(count_ge, shape, topk):
    n_valid = count_ge(jnp.full(shape, INT_MIN + 1, I32))

    def unresolved(cnt):
        return (jnp.max(jnp.where(jnp.logical_and(cnt != topk, n_valid >= topk), 1.0, 0.0)) > 0.0).astype(I32)

    def cond(state):
        b, _, _, go = state
        return jnp.logical_and(b < 32, go > 0)

    def body(state):
        b, lo, cnt_lo, _ = state
        cand = lo + lax.shift_left(jnp.int32(1), jnp.int32(31) - b)
        cnt = count_ge(cand)
        take = cnt >= topk
        lo = jnp.where(take, cand, lo)
        cnt_lo = jnp.where(take, cnt, cnt_lo)
        return b + 1, lo, cnt_lo, unresolved(cnt_lo)

    lo0 = jnp.full(shape, INT_MIN, I32)
    cnt0 = jnp.full(shape, 2.0 ** 31, F32)
    _, thr, n_ge, _ = lax.while_loop(cond, body, (jnp.int32(0), lo0, cnt0, jnp.int32(1)))
    return thr, n_ge


def _tie_index_cut(count_eq_below, need, shape, nbits):
    def body(b, ans):
        cand = ans + lax.shift_left(jnp.int32(1), jnp.int32(nbits - 1) - b)
        return jnp.where(count_eq_below(cand) < need, cand, ans)
    return lax.fori_loop(0, nbits, body, jnp.zeros(shape, I32))


def _idx_kernel(qi_ref, wi_ref, ki_ref, mask_ref, key_scr, wb_scr, *, tq, kc, n_idx, n_chunks_total, topk, nbits):
    i = pl.program_id(0)
    nch = i + 1
    per = kc // LANES
    for h in range(n_idx):
        wb_scr[h] = jnp.broadcast_to(wi_ref[:, h:h + 1], (tq, LANES))
    q_pos = i * tq + lax.broadcasted_iota(I32, (1, tq), 1)
    sub = lax.broadcasted_iota(I32, (kc, 1), 0)

    def score_chunk(c, carry):
        kic = ki_ref[pl.ds(pl.multiple_of(c * kc, kc), kc), :]
        acc = [jnp.zeros((tq, LANES), F32) for _ in range(per)]
        for h in range(n_idx):
            d = lax.dot_general(qi_ref[:, h * LANES:(h + 1) * LANES], kic, _NT, preferred_element_type=F32)
            w = wb_scr[h]
            for j in range(per):
                acc[j] = acc[j] + jnp.maximum(d[:, j * LANES:(j + 1) * LANES], 0.0) * w
        score_t = jnp.concatenate(acc, axis=1).T
        key_scr[c] = jnp.where(c * kc + sub <= q_pos, _sortable_key(score_t), INT_MIN)
        return carry

    lax.fori_loop(0, nch, score_chunk, 0)

    def count(pred):
        def body(c, cnt):
            hit = jnp.where(pred(key_scr[c], c * kc + sub), 1.0, 0.0)
            return cnt + jnp.sum(hit.reshape(kc // 8, 8, tq), axis=0)
        cnt = lax.fori_loop(0, nch, body, jnp.zeros((8, tq), F32))
        return jnp.sum(cnt, axis=0, keepdims=True)

    def write_mask(pred):
        def body(c, carry):
            mask_ref[0, c] = jnp.where(pred(key_scr[c], c * kc + sub), 0.0, NEG).astype(BF16)
            return carry
        lax.fori_loop(0, nch, body, 0)

    thr, n_ge = _kth_largest_key(lambda cand: count(lambda k, kidx: k >= cand), (1, tq), topk)
    tie = jnp.logical_and(n_ge > topk, thr > INT_MIN)
    thr_valid = jnp.maximum(thr, INT_MIN + 1)
    has_tie = jnp.max(jnp.where(tie, 1.0, 0.0)) > 0.0

    def write_fast():
        write_mask(lambda k, kidx: k >= thr_valid)

    def write_ties():
        n_gt = count(lambda k, kidx: k > thr)
        need = topk - n_gt
        cut = _tie_index_cut(lambda cand: count(lambda k, kidx: jnp.logical_and(k == thr, kidx < cand)),
                             need, (1, tq), nbits)
        cut = jnp.where(tie, cut, jnp.int32(2 ** 30))
        write_mask(lambda k, kidx: jnp.logical_and(k >= thr_valid, jnp.logical_or(k > thr, kidx <= cut)))

    lax.cond(has_tie, write_ties, write_fast)

    def fill(c, carry):
        mask_ref[0, c] = jnp.full((kc, tq), NEG, BF16)
        return carry
    lax.fori_loop(nch, n_chunks_total, fill, 0)


def _prompt_select(qi, wi, ki, *, n_idx, name):
    s = qi.shape[0]
    tq = kc = _pick(s, (256, 128))
    nq = s // tq
    topk = min(TOPK_MAX, s // 4)
    nbits = max(1, int(np.ceil(np.log2(s))))
    return pl.pallas_call(
        functools.partial(_idx_kernel, tq=tq, kc=kc, n_idx=n_idx, n_chunks_total=nq, topk=topk, nbits=nbits),
        grid=(nq,),
        in_specs=[pl.BlockSpec((tq, n_idx * LANES), lambda i: (i, 0)),
                  pl.BlockSpec((tq, LANES), lambda i: (i, 0)),
                  pl.BlockSpec((s, LANES), lambda i: (0, 0))],
        out_specs=pl.BlockSpec((1, nq, kc, tq), lambda i: (i, 0, 0, 0)),
        out_shape=jax.ShapeDtypeStruct((nq, nq, kc, tq), BF16),
        scratch_shapes=[pltpu.VMEM((nq, kc, tq), I32), pltpu.VMEM((n_idx, tq, LANES), F32)],
        compiler_params=_cparams(("parallel",)),
        name=name,
    )(qi, wi, ki)


def _attn_kernel(q_ref, k_ref, vt_ref, mask_ref, o_ref, m_scr, l_scr, acc_scr, *, rq, rk, kc, hps, scale2):
    i = pl.program_id(1)
    m_scr[...] = jnp.full(m_scr.shape, NEG, F32)
    l_scr[...] = jnp.zeros(l_scr.shape, F32)
    acc_scr[...] = jnp.zeros(acc_scr.shape, F32)

    def body(c, carry):
        bias = jnp.concatenate(
            [jnp.concatenate([mask_ref[a, c * rk + b] for a in range(rq)], axis=1) for b in range(rk)],
            axis=0).astype(F32)
        rows = pl.ds(pl.multiple_of(c * kc, kc), kc)
        for hh in range(hps):
            sl = slice(hh * LANES, (hh + 1) * LANES)
            s = lax.dot_general(k_ref[rows, sl], q_ref[:, sl], _NT, preferred_element_type=F32) * scale2 + bias
            m = m_scr[hh]
            m_new = jnp.maximum(m, jnp.max(s, axis=0, keepdims=True))
            alpha = jnp.exp2(m - m_new)
            p = jnp.exp2(s - m_new)
            l_scr[hh] = alpha * l_scr[hh] + jnp.sum(p, axis=0, keepdims=True)
            acc_scr[hh] = alpha * acc_scr[hh] + jnp.dot(vt_ref[c, sl, :], p.astype(BF16),
                                                        preferred_element_type=F32)
            m_scr[hh] = m_new
        return carry

    lax.fori_loop(0, i + 1, body, 0)
    for hh in range(hps):
        o_ref[:, hh * LANES:(hh + 1) * LANES] = (acc_scr[hh] / l_scr[hh]).T.astype(o_ref.dtype)


def _prompt_attention(q, k, vt, mask, *, name):
    s, d_b = q.shape
    nqs, nks, kcs, tqs = mask.shape
    nkc, _, kc = vt.shape
    tq = kc
    rq, rk = tq // tqs, kc // kcs
    hps = 2 if (d_b // LANES) % 2 == 0 else 1
    wblk = hps * LANES
    return pl.pallas_call(
        functools.partial(_attn_kernel, rq=rq, rk=rk, kc=kc, hps=hps,
                          scale2=float(LANES ** -0.5 * np.log2(np.e))),
        grid=(d_b // wblk, s // tq),
        in_specs=[pl.BlockSpec((tq, wblk), lambda h, i: (i, h)),
                  pl.BlockSpec((s, wblk), lambda h, i: (0, h)),
                  pl.BlockSpec((nkc, wblk, kc), lambda h, i: (0, h, 0)),
                  pl.BlockSpec((rq, nks, kcs, tqs), lambda h, i: (i, 0, 0, 0))],
        out_specs=pl.BlockSpec((tq, wblk), lambda h, i: (i, h)),
        out_shape=jax.ShapeDtypeStruct((s, d_b), BF16),
        scratch_shapes=[pltpu.VMEM((hps, 1, tq), F32), pltpu.VMEM((hps, 1, tq), F32),
                        pltpu.VMEM((hps, LANES, tq), F32)],
        compiler_params=_cparams(("parallel", "arbitrary")),
        name=name,
    )(q, k, vt, mask)


def _sample_idx_kernel(pt_ref, *refs, g_pages, n_steps, n_idx, dec_seq, topk, nbits, past):
    del pt_ref
    pages = refs[:g_pages]
    qi_ref, wcol_ref, kinew_ref, bias_past_ref, bias_new_ref, kb_scr, key_scr, keyn_scr = refs[g_pages:]
    j = pl.program_id(1)
    pg = g_pages * PAGE_SIZE
    rows = 8
    row_id = lax.broadcasted_iota(I32, (rows, 1), 0)

    def scores_of(kmat):
        d = lax.dot_general(qi_ref[0], kmat, _NT, preferred_element_type=F32)
        wr = jnp.maximum(d, 0.0) * wcol_ref[0]
        s4 = jnp.sum(wr.reshape(dec_seq, n_idx, kmat.shape[0]), axis=1)
        return jnp.concatenate([s4, jnp.zeros((rows - dec_seq, kmat.shape[0]), F32)], axis=0)

    for r in range(g_pages):
        kb_scr[r * PAGE_SIZE:(r + 1) * PAGE_SIZE, :] = pages[r][0].astype(BF16)
    key_scr[j] = jnp.where(row_id < dec_seq, _sortable_key(scores_of(kb_scr[...])), INT_MIN)

    @pl.when(j == n_steps - 1)
    def _():
        sn = scores_of(kinew_ref[0])
        coln = lax.broadcasted_iota(I32, (1, LANES), 1)
        okn = jnp.logical_and(jnp.logical_and(coln < dec_seq, coln <= row_id), row_id < dec_seq)
        keyn_scr[...] = jnp.where(okn, _sortable_key(sn), INT_MIN)

        def count(pred):
            cnt = jnp.zeros((rows, LANES), F32)
            for c in range(n_steps):
                p = pred(key_scr[c], c * pg + lax.broadcasted_iota(I32, (1, pg), 1))
                for jj in range(pg // LANES):
                    cnt = cnt + jnp.where(p[:, jj * LANES:(jj + 1) * LANES], 1.0, 0.0)
            cnt = cnt + jnp.where(pred(keyn_scr[...], past + coln), 1.0, 0.0)
            return jnp.sum(cnt, axis=-1, keepdims=True)

        thr, n_ge = _kth_largest_key(lambda cand: count(lambda k, col: k >= cand), (rows, 1), topk)
        n_gt = count(lambda k, col: k > thr)
        tie = jnp.logical_and(n_ge > topk, thr > INT_MIN)
        thr_valid = jnp.maximum(thr, INT_MIN + 1)
        need = topk - n_gt
        cut = _tie_index_cut(
            lambda cand: count(lambda k, col: jnp.logical_and(k == thr, col < cand)), need, (rows, 1), nbits)
        cut = jnp.where(tie, cut, jnp.int32(2 ** 30))

        def sel(k, col):
            return jnp.logical_and(k >= thr_valid, jnp.logical_or(k > thr, col <= cut))

        half = pg // 2
        for c in range(n_steps):
            b = jnp.where(sel(key_scr[c], c * pg + lax.broadcasted_iota(I32, (1, pg), 1)), 0.0, NEG)
            bias_past_ref[0, 2 * c] = b[:, :half]
            bias_past_ref[0, 2 * c + 1] = b[:, half:]
        bias_new_ref[0] = jnp.where(sel(keyn_scr[...], past + coln), 0.0, NEG)


def _sample_select(page_table, cache_kidx, qi_rows, wcol, kinew, *, n_idx, dec_seq, name):
    bd, n_pages = page_table.shape
    g_pages = _pick(n_pages, (8, 4, 2, 1))
    n_steps = n_pages // g_pages
    pg = g_pages * PAGE_SIZE
    past = n_pages * PAGE_SIZE
    topk = min(TOPK_MAX, (past + dec_seq) // 4)
    nbits = max(1, int(np.ceil(np.log2(past + LANES))))
    rq = dec_seq * n_idx

    def page_spec(r):
        return pl.BlockSpec((1, PAGE_SIZE, LANES), lambda b, j, pt: (pt[b, j * g_pages + r], 0, 0))

    grid_spec = pltpu.PrefetchScalarGridSpec(
        num_scalar_prefetch=1,
        grid=(bd, n_steps),
        in_specs=[page_spec(r) for r in range(g_pages)] + [
            pl.BlockSpec((1, rq, LANES), lambda b, j, pt: (b, 0, 0)),
            pl.BlockSpec((1, rq, 1), lambda b, j, pt: (b, 0, 0)),
            pl.BlockSpec((1, LANES, LANES), lambda b, j, pt: (b, 0, 0))],
        out_specs=[pl.BlockSpec((1, 2 * n_steps, 8, pg // 2), lambda b, j, pt: (b, 0, 0, 0)),
                   pl.BlockSpec((1, 8, LANES), lambda b, j, pt: (b, 0, 0))],
        scratch_shapes=[pltpu.VMEM((pg, LANES), BF16), pltpu.VMEM((n_steps, 8, pg), I32),
                        pltpu.VMEM((8, LANES), I32)],
    )
    return pl.pallas_call(
        functools.partial(_sample_idx_kernel, g_pages=g_pages, n_steps=n_steps, n_idx=n_idx,
                          dec_seq=dec_seq, topk=topk, nbits=nbits, past=past),
        grid_spec=grid_spec,
        out_shape=[jax.ShapeDtypeStruct((bd, 2 * n_steps, 8, pg // 2), F32),
                   jax.ShapeDtypeStruct((bd, 8, LANES), F32)],
        compiler_params=_cparams(("parallel", "arbitrary")),
        name=name,
    )(page_table, *([cache_kidx] * g_pages), qi_rows, wcol, kinew)


def _sample_attn_kernel(pt_ref, *refs, g_pages, n_steps, n_heads, scale):
    del pt_ref
    kpages = refs[:g_pages]
    vpages = refs[g_pages:2 * g_pages]
    (qblk_ref, bias_ref, biasn_ref, knew_ref, vnew_ref, o_ref,
     kb_scr, vb_scr, m_scr, l_scr, acc_scr) = refs[2 * g_pages:]
    j = pl.program_id(1)
    rows = 8 * n_heads

    @pl.when(j == 0)
    def _():
        m_scr[...] = jnp.full(m_scr.shape, NEG, F32)
        l_scr[...] = jnp.zeros(l_scr.shape, F32)
        acc_scr[...] = jnp.zeros(acc_scr.shape, F32)

    def update(kb, vb, bias8):
        st = jnp.dot(kb, qblk_ref[0], preferred_element_type=F32)
        s = st.T * scale + jnp.concatenate([bias8] * n_heads, axis=0)
        m = m_scr[...]
        m_new = jnp.maximum(m, jnp.max(s, axis=-1, keepdims=True))
        alpha = jnp.exp2(m - m_new)
        p = jnp.exp2(s - m_new)
        l_scr[...] = alpha * l_scr[...] + jnp.sum(p, axis=-1, keepdims=True)
        acc_scr[...] = alpha * acc_scr[...] + jnp.dot(p.astype(BF16), vb, preferred_element_type=F32)
        m_scr[...] = m_new

    for r in range(g_pages):
        rs = slice(r * PAGE_SIZE, (r + 1) * PAGE_SIZE)
        kh = pltpu.einshape("khd->hkd", kpages[r][0, 0].reshape(PAGE_SIZE, n_heads, LANES))
        vh = pltpu.einshape("khd->hkd", vpages[r][0, 0].reshape(PAGE_SIZE, n_heads, LANES))
        for h in range(n_heads):
            cs = slice(h * LANES, (h + 1) * LANES)
            kb_scr[rs, cs] = kh[h].astype(BF16)
            vb_scr[rs, cs] = vh[h].astype(BF16)
    update(kb_scr[...], vb_scr[...], bias_ref[0, 0])

    @pl.when(j == n_steps - 1)
    def _():
        update(knew_ref[0], vnew_ref[0], biasn_ref[0])
        for h in range(n_heads):
            rs = slice(h * 8, (h + 1) * 8)
            cs = slice(h * LANES, (h + 1) * LANES)
            o_ref[0, :, cs] = acc_scr[rs, cs] / l_scr[rs, :]


def _sample_attention(page_table, cache_k, cache_v, layer, qblk, bias_past, bias_new, knew, vnew, *, n_heads, name):
    bd, n_pages = page_table.shape
    d_b = n_heads * LANES
    n_steps = bias_past.shape[1]
    g_pages = n_pages // n_steps
    pg = g_pages * PAGE_SIZE
    rows = 8 * n_heads

    def page_spec(r):
        return pl.BlockSpec((1, 1, PAGE_SIZE * n_heads, LANES),
                            lambda b, j, pt: (layer, pt[b, j * g_pages + r], 0, 0))

    grid_spec = pltpu.PrefetchScalarGridSpec(
        num_scalar_prefetch=1,
        grid=(bd, n_steps),
        in_specs=[page_spec(r) for r in range(g_pages)] * 2 + [
            pl.BlockSpec((1, d_b, rows), lambda b, j, pt: (b, 0, 0)),
            pl.BlockSpec((1, 1, 8, pg), lambda b, j, pt: (b, j, 0, 0)),
            pl.BlockSpec((1, 8, LANES), lambda b, j, pt: (b, 0, 0)),
            pl.BlockSpec((1, LANES, d_b), lambda b, j, pt: (b, 0, 0)),
            pl.BlockSpec((1, LANES, d_b), lambda b, j, pt: (b, 0, 0))],
        out_specs=pl.BlockSpec((1, 8, d_b), lambda b, j, pt: (b, 0, 0)),
        scratch_shapes=[pltpu.VMEM((pg, d_b), BF16), pltpu.VMEM((pg, d_b), BF16),
                        pltpu.VMEM((rows, 1), F32), pltpu.VMEM((rows, 1), F32), pltpu.VMEM((rows, d_b), F32)],
    )
    return pl.pallas_call(
        functools.partial(_sample_attn_kernel, g_pages=g_pages, n_steps=n_steps, n_heads=n_heads,
                          scale=float(LANES ** -0.5 * np.log2(np.e))),
        grid_spec=grid_spec,
        out_shape=jax.ShapeDtypeStruct((bd, 8, d_b), F32),
        compiler_params=_cparams(("parallel", "arbitrary")),
        name=name,
    )(page_table, *([cache_k] * g_pages), *([cache_v] * g_pages), qblk, bias_past, bias_new, knew, vnew)


def _merge_kernel(h_ref, a_ref, attn_ref, zs_ref, wga_ref, wgb_ref, wua_ref, wub_ref, o_ref, bg_scr):
    @pl.when(pl.program_id(1) == 0)
    def _():
        bg_scr[...] = (attn_ref[...].astype(F32) * zs_ref[...].astype(F32)).astype(BF16)

    h = h_ref[...]
    ga = jnp.dot(h, wga_ref[...], preferred_element_type=F32)
    gb = jnp.dot(h, wgb_ref[...], preferred_element_type=F32)
    a_up = jnp.dot(a_ref[...], wua_ref[...], preferred_element_type=F32)
    b_up = jnp.dot(bg_scr[...], wub_ref[...], preferred_element_type=F32)
    o_ref[...] = (_sigmoid(ga) * a_up + _sigmoid(gb) * b_up).astype(o_ref.dtype)


def _merge(h, a_gated, attn, zs, w_ga, w_gb, w_ua, w_ub, *, name):
    m, d = h.shape
    d_a, d_b = a_gated.shape[1], attn.shape[1]
    tm = _pick(m, (512, 256, 128))
    tn = _pick(d, (256, 128))
    row = lambda i, j: (i, 0)
    col = lambda i, j: (0, j)
    return pl.pallas_call(
        _merge_kernel,
        grid=(m // tm, d // tn),
        in_specs=[pl.BlockSpec((tm, d), row), pl.BlockSpec((tm, d_a), row),
                  pl.BlockSpec((tm, d_b), row), pl.BlockSpec((tm, d_b), row),
                  pl.BlockSpec((d, tn), col), pl.BlockSpec((d, tn), col),
                  pl.BlockSpec((d_a, tn), col), pl.BlockSpec((d_b, tn), col)],
        out_specs=pl.BlockSpec((tm, tn), lambda i, j: (i, j)),
        out_shape=jax.ShapeDtypeStruct((m, d), BF16),
        scratch_shapes=[pltpu.VMEM((tm, d_b), BF16)],
        compiler_params=_cparams(("parallel", "arbitrary")),
        name=name,
    )(h, a_gated, attn, zs, w_ga, w_gb, w_ua, w_ub)


def _resid_kernel(mg_ref, w_ref, x_ref, o_ref):
    o_ref[...] = x_ref[...] + jnp.dot(mg_ref[...], w_ref[...], preferred_element_type=F32)


def _resid_proj(merged, w_o, x, *, name):
    m, d = x.shape
    tm = _pick(m, (1024, 512, 256, 128))
    tn = _pick(d, (512, 256, 128))
    return pl.pallas_call(
        _resid_kernel,
        grid=(m // tm, d // tn),
        in_specs=[pl.BlockSpec((tm, merged.shape[1]), lambda i, j: (i, 0)),
                  pl.BlockSpec((merged.shape[1], tn), lambda i, j: (0, j)),
                  pl.BlockSpec((tm, tn), lambda i, j: (i, j))],
        out_specs=pl.BlockSpec((tm, tn), lambda i, j: (i, j)),
        out_shape=jax.ShapeDtypeStruct((m, d), F32),
        compiler_params=_cparams(("parallel", "arbitrary")),
        name=name,
    )(merged, w_o, x)


def _ple_kernel(hn_ref, wg_ref, p_ref, wp_ref, x1_ref, o_ref):
    gate = _sigmoid(jnp.dot(hn_ref[...], wg_ref[...], preferred_element_type=F32))
    emb = jnp.dot(p_ref[...], wp_ref[...], preferred_element_type=F32)
    o_ref[...] = x1_ref[...] + gate * emb


def _ple(hn, w_gate, p, w_proj, x1, *, name):
    m, d = x1.shape
    tm = _pick(m, (1024, 512, 256, 128))
    tn = _pick(d, (512, 256, 128))
    return pl.pallas_call(
        _ple_kernel,
        grid=(m // tm, d // tn),
        in_specs=[pl.BlockSpec((tm, d), lambda i, j: (i, 0)),
                  pl.BlockSpec((d, tn), lambda i, j: (0, j)),
                  pl.BlockSpec((tm, p.shape[1]), lambda i, j: (i, 0)),
                  pl.BlockSpec((p.shape[1], tn), lambda i, j: (0, j)),
                  pl.BlockSpec((tm, tn), lambda i, j: (i, j))],
        out_specs=pl.BlockSpec((tm, tn), lambda i, j: (i, j)),
        out_shape=jax.ShapeDtypeStruct((m, d), F32),
        compiler_params=_cparams(("parallel", "arbitrary")),
        name=name,
    )(hn, w_gate, p, w_proj, x1)


def _rope_tables(pos):
    half = LANES // 2
    inv = ROPE_THETA ** (-jnp.arange(half, dtype=F32) / half)
    ang = pos.astype(F32)[:, None] * inv[None, :]
    cos, sin = jnp.cos(ang), jnp.sin(ang)
    return jnp.concatenate([cos, cos], axis=-1), jnp.concatenate([-sin, sin], axis=-1)


def _split_weights(w_in, d_model, d_a, d_b, n_idx):
    widths = [3 * d_a, d_b, d_b, d_b, d_b, n_idx * LANES, LANES + n_idx, d_model, d_model]
    offs = np.concatenate([[0], np.cumsum(widths)])
    parts = [w_in[:, int(offs[i]):int(offs[i + 1])].astype(BF16) for i in range(len(widths))]
    parts[6] = jnp.pad(parts[6], ((0, 0), (0, 2 * LANES - parts[6].shape[1])))
    return parts


def _layer(x, p, pos, tag, w, ws, bs, *, n_heads, n_idx, emit_vn, v_chunk=0):
    m, d_model = x.shape
    cosf, sinf = _rope_tables(pos)
    h = _rmsnorm(x, w["g_norm"], f"{tag}_norm")
    a_out = _branch_a(h, w["w_a"], w["g_va"], w["b_va"], ws, bs, emit_vn=emit_vn, name=f"{tag}_branch_a")
    gq = jnp.tile(w["g_q"], n_heads).reshape(1, -1)
    gk = jnp.tile(w["g_k"], n_heads).reshape(1, -1)
    (q_bf,) = _proj(h, w["w_q"], kind="norm_rope", n_f32=0, n_bf16=1, gain=gq, cos=cosf, sin=sinf, name=f"{tag}_q")
    k_f32, k_bf = _proj(h, w["w_k"], kind="norm_rope", n_f32=1, n_bf16=1, gain=gk, cos=cosf, sin=sinf,
                        name=f"{tag}_k")
    if v_chunk:
        v_f32, v_bf = _proj(h, w["w_v"], kind="plain", n_f32=1, n_bf16=0, t_chunk=v_chunk, name=f"{tag}_v")
    else:
        v_f32, v_bf = _proj(h, w["w_v"], kind="plain", n_f32=1, n_bf16=1, name=f"{tag}_v")
    (zs,) = _proj(h, w["w_zb"], kind="silu", n_f32=0, n_bf16=1, name=f"{tag}_zb")
    (qi,) = _proj(h, w["w_qi"], kind="rope", n_f32=0, n_bf16=1, cos=cosf, sin=sinf, name=f"{tag}_qi")
    ki_f32, ki_bf, wi = _proj(h, w["w_kiwi"], kind="kiwi", n_f32=0, n_bf16=0, cos=cosf, sin=sinf,
                              scale=float(n_idx ** -0.5 * LANES ** -0.5), name=f"{tag}_kiwi")
    return dict(h=h, a=a_out, q=q_bf, k=k_f32, kb=k_bf, v=v_f32, vb=v_bf, zs=zs, qi=qi, ki=ki_f32, kib=ki_bf, wi=wi)


def _finish(x, p, t, attn, tag, w):
    merged = _merge(t["h"], t["a"][0], attn, t["zs"], w["w_ga"], w["w_gb"], w["w_ua"], w["w_ub"], name=f"{tag}_merge")
    x1 = _resid_proj(merged, w["w_o"], x, name=f"{tag}_resid")
    hn = _rmsnorm(x1, w["g_ple"], f"{tag}_ple_norm")
    return _ple(hn, w["w_pg"], p.astype(BF16), w["w_pp"], x1, name=f"{tag}_ple")


def kernel(x_prompt, x_sample, p_prompt, p_sample, cache_k, cache_v, cache_kidx, page_table, g_norm, w_in, g_va,
           b_va, w_s, b_s, g_q, g_k, w_up_a, w_up_b, w_o, g_ple, w_ple_gate, w_ple_proj):
    depth = w_in.shape[0]
    batch, seq, d_model = x_prompt.shape
    bd, dec_seq, _ = x_sample.shape
    n_phys, _, n_heads, head_dim = cache_k.shape[1:]
    n_groups = w_s.shape[1]
    d_a, d_b = w_up_a.shape[1], w_up_b.shape[1]
    n_idx = w_in.shape[2] - (3 * d_a + 4 * d_b + LANES + 2 * d_model)
    n_idx = n_idx // (LANES + 1)
    n_pages = page_table.shape[1]
    past = n_pages * PAGE_SIZE
    assert batch == 1 and head_dim == LANES and cache_kidx.shape[-1] == LANES and d_a == n_groups * LANES
    assert seq % CHUNK == 0 and bd * dec_seq == CHUNK and dec_seq <= 8 and d_b == n_heads * LANES
    ms = bd * dec_seq

    yp = x_prompt.reshape(seq, d_model)
    ys = x_sample.reshape(ms, d_model)
    pos_p = jnp.arange(seq)
    pos_s = jnp.tile(past + jnp.arange(dec_seq), bd)
    outs = {k: [] for k in ("kp", "vp", "kip", "ks", "vs", "kis", "vas")}
    tril = jnp.tril(jnp.ones((CHUNK, CHUNK), bool))
    cache_k2 = cache_k.reshape(depth, n_phys, PAGE_SIZE * n_heads, LANES)
    cache_v2 = cache_v.reshape(depth, n_phys, PAGE_SIZE * n_heads, LANES)
    for i in range(depth):
        parts = _split_weights(w_in[i], d_model, d_a, d_b, n_idx)
        w = dict(zip(("w_a", "w_q", "w_k", "w_v", "w_zb", "w_qi", "w_kiwi", "w_ga", "w_gb"), parts))
        w.update(g_norm=g_norm[i], g_va=g_va[i], b_va=b_va[i], g_q=g_q[i], g_k=g_k[i], g_ple=g_ple[i],
                 w_ua=w_up_a[i].astype(BF16), w_ub=w_up_b[i].astype(BF16), w_o=w_o[i].astype(BF16),
                 w_pg=w_ple_gate[i].astype(BF16), w_pp=w_ple_proj[i].astype(BF16))
        ws_p = jnp.where(tril[None], w_s[i], 0.0).astype(BF16)
        bs_p = jnp.broadcast_to(b_s[i][:, :, None], (n_groups, CHUNK, LANES))
        small = jnp.where(tril[None, :dec_seq, :dec_seq], w_s[i][:, :dec_seq, :dec_seq], 0.0)
        ws_s = jnp.einsum("ab,gts->gatbs", jnp.eye(bd, dtype=F32), small).reshape(n_groups, ms, ms).astype(BF16)
        bs_s = jnp.broadcast_to(jnp.tile(b_s[i][:, :dec_seq], (1, bd))[:, :, None], (n_groups, ms, LANES))

        tp = _layer(yp, p_prompt[i, 0], pos_p, "prompt", w, ws_p, bs_p, n_heads=n_heads, n_idx=n_idx, emit_vn=False,
                    v_chunk=_pick(seq, (512, 256, 128)))
        mask = _prompt_select(tp["qi"], tp["wi"], tp["kib"], n_idx=n_idx, name="prompt_select")
        attn_p = _prompt_attention(tp["q"], tp["kb"], tp["vb"], mask, name="prompt_attention")
        yp_new = _finish(yp, p_prompt[i, 0], tp, attn_p, "prompt", w)

        tsm = _layer(ys, p_sample[i].reshape(ms, -1), pos_s, "sample", w, ws_s, bs_s, n_heads=n_heads, n_idx=n_idx,
                     emit_vn=True)
        qi_rows = tsm["qi"].reshape(bd, dec_seq * n_idx, LANES)
        wcol = tsm["wi"][:, :n_idx].reshape(bd, dec_seq * n_idx, 1)
        kinew = jnp.pad(tsm["kib"].reshape(bd, dec_seq, LANES), ((0, 0), (0, LANES - dec_seq), (0, 0)))
        bias_past, bias_new = _sample_select(page_table, cache_kidx[i], qi_rows, wcol, kinew, n_idx=n_idx,
                                             dec_seq=dec_seq, name="sample_select")
        q4 = jnp.pad(tsm["q"].reshape(bd, dec_seq, n_heads, LANES), ((0, 0), (0, 8 - dec_seq), (0, 0), (0, 0)))
        qblk = jnp.einsum("bthd,hg->bhdgt", q4, jnp.eye(n_heads, dtype=BF16)).reshape(bd, d_b, n_heads * 8)
        padrows = ((0, 0), (0, LANES - dec_seq), (0, 0))
        knew = jnp.pad(tsm["kb"].reshape(bd, dec_seq, d_b), padrows)
        vnew = jnp.pad(tsm["vb"].reshape(bd, dec_seq, d_b), padrows)
        attn_s8 = _sample_attention(page_table, cache_k2, cache_v2, i, qblk, bias_past, bias_new,
                                    knew, vnew, n_heads=n_heads, name="sample_attention")
        attn_s = attn_s8[:, :dec_seq].reshape(ms, d_b).astype(BF16)
        ys_new = _finish(ys, p_sample[i].reshape(ms, -1), tsm, attn_s, "sample", w)

        outs["kp"].append(tp["k"].reshape(batch, seq, n_heads, LANES))
        outs["vp"].append(tp["v"].reshape(batch, seq, n_heads, LANES))
        outs["kip"].append(tp["ki"].reshape(batch, seq, LANES))
        outs["ks"].append(tsm["k"].reshape(bd, dec_seq, n_heads, LANES))
        outs["vs"].append(tsm["v"].reshape(bd, dec_seq, n_heads, LANES))
        outs["kis"].append(tsm["ki"].reshape(bd, dec_seq, LANES))
        outs["vas"].append(tsm["a"][1].reshape(bd, dec_seq, d_a))
        yp, ys = yp_new, ys_new

    return (yp.reshape(batch, seq, d_model), ys.reshape(bd, dec_seq, d_model),
            jnp.stack(outs["kp"]), jnp.stack(outs["vp"]), jnp.stack(outs["kip"]),
            jnp.stack(outs["ks"]), jnp.stack(outs["vs"]), jnp.stack(outs["kis"]), jnp.stack(outs["vas"]))
```

```python
import functools

import jax
import jax.numpy as jnp
import numpy as np
from jax import lax
from jax.experimental import pallas as pl
from jax.experimental.pallas import tpu as pltpu

F32 = jnp.float32
BF16 = jnp.bfloat16
I32 = jnp.int32

LANES = 128
PAGE_SIZE = 128
CHUNK = 128
TOPK_MAX = 256
ROPE_THETA = 10000.0
EPS = 1e-6
NEG = -1e30
INT_MIN = -(2 ** 31)
VMEM_LIMIT = 56 * 1024 * 1024

_NT = (((1,), (1,)), ((), ()))


def _cparams(sem):
    return pltpu.CompilerParams(dimension_semantics=sem, vmem_limit_bytes=VMEM_LIMIT)


def _sigmoid(x):
    return 1.0 / (1.0 + jnp.exp(-x))


def _pick(n, cands):
    for c in cands:
        if n % c == 0:
            return c
    return n


def _rmsnorm_kernel(x_ref, g_ref, o_ref):
    x = x_ref[...]
    ms = jnp.mean(x * x, axis=-1, keepdims=True)
    o_ref[...] = (x * lax.rsqrt(ms + EPS) * g_ref[...]).astype(o_ref.dtype)


def _rmsnorm(x, g, name):
    m, d = x.shape
    tm = _pick(m, (256, 128))
    return pl.pallas_call(
        _rmsnorm_kernel,
        grid=(m // tm,),
        in_specs=[pl.BlockSpec((tm, d), lambda i: (i, 0)),
                  pl.BlockSpec((1, d), lambda i: (0, 0))],
        out_specs=pl.BlockSpec((tm, d), lambda i: (i, 0)),
        out_shape=jax.ShapeDtypeStruct((m, d), BF16),
        compiler_params=_cparams(("parallel",)),
        name=name,
    )(x, g.reshape(1, d))


def _rope(x, cos, sin_signed):
    return x * cos + pltpu.roll(x, LANES // 2, axis=1) * sin_signed


def _proj_kernel(*refs, kind, n_f32, n_bf16, tn, scale, t_chunk):
    h_ref, w_ref = refs[0], refs[1]
    acc = jnp.dot(h_ref[...], w_ref[...], preferred_element_type=F32)
    if t_chunk:
        t_ref, refs = refs[-1], refs[:-1]
        for a in range(acc.shape[0] // t_chunk):
            t_ref[a] = acc[a * t_chunk:(a + 1) * t_chunk, :].T.astype(BF16)
    pos = 2
    if kind == "norm_rope":
        g_ref, cos_ref, sin_ref = refs[pos:pos + 3]
        pos += 3
    elif kind in ("rope", "kiwi"):
        cos_ref, sin_ref = refs[pos:pos + 2]
        pos += 2
    outs = refs[pos:]
    if kind == "kiwi":
        ki = _rope(acc[:, :LANES], cos_ref[...], sin_ref[...])
        outs[0][...] = ki
        outs[1][...] = ki.astype(BF16)
        outs[2][...] = acc[:, LANES:] * scale
        return
    for j in range(tn // LANES):
        sl = slice(j * LANES, (j + 1) * LANES)
        x = acc[:, sl]
        if kind == "norm_rope":
            ms = jnp.mean(x * x, axis=-1, keepdims=True)
            x = x * lax.rsqrt(ms + EPS) * g_ref[:, sl]
            x = _rope(x, cos_ref[...], sin_ref[...])
        elif kind == "rope":
            x = _rope(x, cos_ref[...], sin_ref[...])
        elif kind == "silu":
            x = x * _sigmoid(x)
        k = 0
        for _ in range(n_f32):
            outs[k][:, sl] = x
            k += 1
        for _ in range(n_bf16):
            outs[k][:, sl] = x.astype(BF16)
            k += 1


def _proj(h, w, *, kind, n_f32, n_bf16, name, col_off, n, gain=None, cos=None, sin=None, scale=1.0, t_chunk=0):
    m, kdim = h.shape
    tm = _pick(m, (1024, 512, 256, 128))
    assert t_chunk == 0 or tm % t_chunk == 0
    tn = 256 if kind == "kiwi" else _pick(np.gcd(n, col_off) if col_off else n, (512, 256, 128))
    assert col_off % tn == 0 and n % tn == 0
    blk_off = col_off // tn
    in_specs = [pl.BlockSpec((tm, kdim), lambda i, j: (i, 0)),
                pl.BlockSpec((kdim, tn), lambda i, j: (0, j + blk_off))]
    args = [h, w]
    if kind == "norm_rope":
        in_specs.append(pl.BlockSpec((1, tn), lambda i, j: (0, j)))
        args.append(gain)
    if kind in ("norm_rope", "rope", "kiwi"):
        in_specs += [pl.BlockSpec((tm, LANES), lambda i, j: (i, 0))] * 2
        args += [cos, sin]
    if kind == "kiwi":
        out_shape = [jax.ShapeDtypeStruct((m, LANES), F32), jax.ShapeDtypeStruct((m, LANES), BF16),
                     jax.ShapeDtypeStruct((m, LANES), F32)]
        out_specs = [pl.BlockSpec((tm, LANES), lambda i, j: (i, 0))] * 3
    else:
        out_shape = ([jax.ShapeDtypeStruct((m, n), F32)] * n_f32 + [jax.ShapeDtypeStruct((m, n), BF16)] * n_bf16)
        out_specs = [pl.BlockSpec((tm, tn), lambda i, j: (i, j))] * (n_f32 + n_bf16)
    if t_chunk:
        out_shape = out_shape + [jax.ShapeDtypeStruct((m // t_chunk, n, t_chunk), BF16)]
        out_specs = out_specs + [pl.BlockSpec((tm // t_chunk, tn, t_chunk), lambda i, j: (i, j, 0))]
    return pl.pallas_call(
        functools.partial(_proj_kernel, kind=kind, n_f32=n_f32, n_bf16=n_bf16, tn=tn, scale=scale,
                          t_chunk=t_chunk),
        grid=(m // tm, n // tn),
        in_specs=in_specs,
        out_specs=out_specs,
        out_shape=out_shape,
        compiler_params=_cparams(("parallel", "arbitrary")),
        name=name,
    )(*args)


def _branch_a_kernel(h_ref, w_ref, gva_ref, bva_ref, ws_ref, bs_ref, *rest, tm, tn, d_a, emit_vn):
    if emit_vn:
        a_ref, vn_ref, scr = rest
    else:
        a_ref, scr = rest
    n = pl.program_id(1)
    nseg = d_a // tn
    per = tn // LANES
    scr[n] = jnp.dot(h_ref[...], w_ref[...], preferred_element_type=F32)

    @pl.when(n == 3 * nseg - 1)
    def _():
        s1 = jnp.zeros((tm, 1), F32)
        for j in range(nseg):
            s1 = s1 + jnp.sum(scr[nseg + j], axis=-1, keepdims=True)
        mu = s1 / d_a
        s2 = jnp.zeros((tm, 1), F32)
        for j in range(nseg):
            dlt = scr[nseg + j] - mu
            s2 = s2 + jnp.sum(dlt * dlt, axis=-1, keepdims=True)
        rstd = lax.rsqrt(s2 / d_a + EPS)
        for g in range(d_a // LANES):
            t_idx, sl = g // per, slice((g % per) * LANES, (g % per + 1) * LANES)
            gsl = slice(g * LANES, (g + 1) * LANES)
            vn = (scr[nseg + t_idx][:, sl] - mu) * rstd * gva_ref[:, gsl] + bva_ref[:, gsl]
            if emit_vn:
                vn_ref[:, gsl] = vn
            vb = vn.astype(BF16)
            u = scr[t_idx][:, sl]
            z = scr[2 * nseg + t_idx][:, sl]
            for c in range(tm // CHUNK):
                rs = slice(c * CHUNK, (c + 1) * CHUNK)
                mixed = jnp.dot(ws_ref[g], vb[rs], preferred_element_type=F32) + bs_ref[g]
                a_ref[rs, gsl] = (u[rs] * mixed * (z[rs] * _sigmoid(z[rs]))).astype(BF16)


def _branch_a(h, w_a, g_va, b_va, ws, bs, *, emit_vn, name):
    m, kdim = h.shape
    d_a = g_va.shape[0]
    tm = _pick(m, (512, 256, 128))
    tn = _pick(d_a, (512, 256, 128))
    ng = d_a // LANES
    out_shape = [jax.ShapeDtypeStruct((m, d_a), BF16)]
    out_specs = [pl.BlockSpec((tm, d_a), lambda i, j: (i, 0))]
    if emit_vn:
        out_shape.append(jax.ShapeDtypeStruct((m, d_a), F32))
        out_specs.append(pl.BlockSpec((tm, d_a), lambda i, j: (i, 0)))
    return pl.pallas_call(
        functools.partial(_branch_a_kernel, tm=tm, tn=tn, d_a=d_a, emit_vn=emit_vn),
        grid=(m // tm, 3 * d_a // tn),
        in_specs=[pl.BlockSpec((tm, kdim), lambda i, j: (i, 0)),
                  pl.BlockSpec((kdim, tn), lambda i, j: (0, j)),
                  pl.BlockSpec((1, d_a), lambda i, j: (0, 0)),
                  pl.BlockSpec((1, d_a), lambda i, j: (0, 0)),
                  pl.BlockSpec((ng, CHUNK, CHUNK), lambda i, j: (0, 0, 0)),
                  pl.BlockSpec((ng, CHUNK, LANES), lambda i, j: (0, 0, 0))],
        out_specs=out_specs,
        out_shape=out_shape,
        scratch_shapes=[pltpu.VMEM((3 * d_a // tn, tm, tn), F32)],
        compiler_params=_cparams(("parallel", "arbitrary")),
        name=name,
    )(h, w_a, g_va.reshape(1, d_a), b_va.reshape(1, d_a), ws, bs)


def _sortable_key(score):
    bits = lax.bitcast_convert_type(score, I32)
    return jnp.where(bits < 0, (bits ^ 0x7FFFFFFF) + 1, bits)


def _kth_largest_key(count_ge, shape, topk):
    n_valid = count_ge(jnp.full(shape, INT_MIN + 1, I32))

    def unresolved(cnt):
        return (jnp.max(jnp.where(jnp.logical_and(cnt != topk, n_valid >= topk), 1.0, 0.0)) > 0.0).astype(I32)

    def cond(state):
        b, _, _, go = state
        return jnp.logical_and(b < 32, go > 0)

    def body(state):
        b, lo, cnt_lo, _ = state
        cand = lo + lax.shift_left(jnp.int32(1), jnp.int32(31) - b)
        cnt = count_ge(cand)
        take = cnt >= topk
        lo = jnp.where(take, cand, lo)
        cnt_lo = jnp.where(take, cnt, cnt_lo)
        return b + 1, lo, cnt_lo, unresolved(cnt_lo)

    lo0 = jnp.full(shape, INT_MIN, I32)
    cnt0 = jnp.full(shape, 2.0 ** 31, F32)
    _, thr, n_ge, _ = lax.while_loop(cond, body, (jnp.int32(0), lo0, cnt0, jnp.int32(1)))
    return thr, n_ge


def _tie_index_cut(count_eq_below, need, shape, nbits):
    def body(b, ans):
        cand = ans + lax.shift_left(jnp.int32(1), jnp.int32(nbits - 1) - b)
        return jnp.where(count_eq_below(cand) < need, cand, ans)
    return lax.fori_loop(0, nbits, body, jnp.zeros(shape, I32))


def _idx_kernel(qi_ref, wi_ref, ki_ref, mask_ref, key_scr, wb_scr, *, tq, kc, n_idx, n_chunks_total, topk, nbits):
    i = pl.program_id(0)
    nch = i + 1
    per = kc // LANES
    for h in range(n_idx):
        wb_scr[h] = jnp.broadcast_to(wi_ref[:, h:h + 1], (tq, LANES))
    q_pos = i * tq + lax.broadcasted_iota(I32, (1, tq), 1)
    sub = lax.broadcasted_iota(I32, (kc, 1), 0)

    def score_chunk(c, carry):
        kic = ki_ref[pl.ds(pl.multiple_of(c * kc, kc), kc), :]
        acc = [jnp.zeros((tq, LANES), F32) for _ in range(per)]
        for h in range(n_idx):
            d = lax.dot_general(qi_ref[:, h * LANES:(h + 1) * LANES], kic, _NT, preferred_element_type=F32)
            w = wb_scr[h]
            for j in range(per):
                acc[j] = acc[j] + jnp.maximum(d[:, j * LANES:(j + 1) * LANES], 0.0) * w
        score_t = jnp.concatenate(acc, axis=1).T
        key_scr[c] = jnp.where(c * kc + sub <= q_pos, _sortable_key(score_t), INT_MIN)
        return carry

    lax.fori_loop(0, nch, score_chunk, 0)

    def count(pred):
        def body(c, cnt):
            hit = jnp.where(pred(key_scr[c], c * kc + sub), 1.0, 0.0)
            return cnt + jnp.sum(hit.reshape(kc // 8, 8, tq), axis=0)
        cnt = lax.fori_loop(0, nch, body, jnp.zeros((8, tq), F32))
        return jnp.sum(cnt, axis=0, keepdims=True)

    def write_mask(pred):
        def body(c, carry):
            mask_ref[0, c] = jnp.where(pred(key_scr[c], c * kc + sub), 0.0, NEG).astype(BF16)
            return carry
        lax.fori_loop(0, nch, body, 0)

    thr, n_ge = _kth_largest_key(lambda cand: count(lambda k, kidx: k >= cand), (1, tq), topk)
    tie = jnp.logical_and(n_ge > topk, thr > INT_MIN)
    thr_valid = jnp.maximum(thr, INT_MIN + 1)
    has_tie = jnp.max(jnp.where(tie, 1.0, 0.0)) > 0.0

    def write_fast():
        write_mask(lambda k, kidx: k >= thr_valid)

    def write_ties():
        n_gt = count(lambda k, kidx: k > thr)
        need = topk - n_gt
        cut = _tie_index_cut(lambda cand: count(lambda k, kidx: jnp.logical_and(k == thr, kidx < cand)),
                             need, (1, tq), nbits)
        cut = jnp.where(tie, cut, jnp.int32(2 ** 30))
        write_mask(lambda k, kidx: jnp.logical_and(k >= thr_valid, jnp.logical_or(k > thr, kidx <= cut)))

    lax.cond(has_tie, write_ties, write_fast)

    def fill(c, carry):
        mask_ref[0, c] = jnp.full((kc, tq), NEG, BF16)
        return carry
    lax.fori_loop(nch, n_chunks_total, fill, 0)


def _prompt_select(qi, wi, ki, *, n_idx, name):
    s = qi.shape[0]
    tq = kc = _pick(s, (256, 128))
    nq = s // tq
    topk = min(TOPK_MAX, s // 4)
    nbits = max(1, int(np.ceil(np.log2(s))))
    return pl.pallas_call(
        functools.partial(_idx_kernel, tq=tq, kc=kc, n_idx=n_idx, n_chunks_total=nq, topk=topk, nbits=nbits),
        grid=(nq,),
        in_specs=[pl.BlockSpec((tq, n_idx * LANES), lambda i: (i, 0)),
                  pl.BlockSpec((tq, LANES), lambda i: (i, 0)),
                  pl.BlockSpec((s, LANES), lambda i: (0, 0))],
        out_specs=pl.BlockSpec((1, nq, kc, tq), lambda i: (i, 0, 0, 0)),
        out_shape=jax.ShapeDtypeStruct((nq, nq, kc, tq), BF16),
        scratch_shapes=[pltpu.VMEM((nq, kc, tq), I32), pltpu.VMEM((n_idx, tq, LANES), F32)],
        compiler_params=_cparams(("parallel",)),
        name=name,
    )(qi, wi, ki)


def _attn_kernel(q_ref, k_ref, vt_ref, mask_ref, o_ref, m_scr, acc_scr, *, rq, rk, kc, hps, scale2):
    i = pl.program_id(1)
    m_scr[...] = jnp.full(m_scr.shape, NEG, F32)
    acc_scr[...] = jnp.zeros(acc_scr.shape, F32)
    ones = jnp.ones((acc_scr.shape[1] - LANES, kc), BF16)

    def body(c, carry):
        bias = jnp.concatenate(
            [jnp.concatenate([mask_ref[a, c * rk + b] for a in range(rq)], axis=1) for b in range(rk)],
            axis=0).astype(F32)
        rows = pl.ds(pl.multiple_of(c * kc, kc), kc)
        heads = [slice(hh * LANES, (hh + 1) * LANES) for hh in range(hps)]
        logits = [lax.dot_general(k_ref[rows, sl], q_ref[:, sl], _NT, preferred_element_type=F32) for sl in heads]
        for hh, sl in enumerate(heads):
            s = logits[hh] * scale2 + bias
            m = m_scr[hh]
            m_new = jnp.maximum(m, jnp.max(s, axis=0, keepdims=True))
            alpha = jnp.exp2(m - m_new)
            p = jnp.exp2(s - m_new).astype(BF16)
            v_ext = jnp.concatenate([vt_ref[c, sl, :], ones], axis=0)
            acc_scr[hh] = alpha * acc_scr[hh] + jnp.dot(v_ext, p, preferred_element_type=F32)
            m_scr[hh] = m_new
        return carry

    lax.fori_loop(0, i + 1, body, 0)
    for hh in range(hps):
        acc = acc_scr[hh]
        o_ref[:, hh * LANES:(hh + 1) * LANES] = (acc[:LANES] / acc[LANES:LANES + 1]).T.astype(o_ref.dtype)


def _prompt_attention(q, k, vt, mask, *, name):
    s, d_b = q.shape
    nqs, nks, kcs, tqs = mask.shape
    nkc, _, kc = vt.shape
    tq = kc
    rq, rk = tq // tqs, kc // kcs
    hps = _pick(d_b // LANES, (4, 2, 1))
    wblk = hps * LANES
    once = pl.Buffered(1)
    return pl.pallas_call(
        functools.partial(_attn_kernel, rq=rq, rk=rk, kc=kc, hps=hps,
                          scale2=float(LANES ** -0.5 * np.log2(np.e))),
        grid=(d_b // wblk, s // tq),
        in_specs=[pl.BlockSpec((tq, wblk), lambda h, i: (i, h)),
                  pl.BlockSpec((s, wblk), lambda h, i: (0, h), pipeline_mode=once),
                  pl.BlockSpec((nkc, wblk, kc), lambda h, i: (0, h, 0), pipeline_mode=once),
                  pl.BlockSpec((rq, nks, kcs, tqs), lambda h, i: (i, 0, 0, 0))],
        out_specs=pl.BlockSpec((tq, wblk), lambda h, i: (i, h)),
        out_shape=jax.ShapeDtypeStruct((s, d_b), BF16),
        scratch_shapes=[pltpu.VMEM((hps, 1, tq), F32), pltpu.VMEM((hps, LANES + 16, tq), F32)],
        compiler_params=_cparams(("parallel", "arbitrary")),
        name=name,
    )(q, k, vt, mask)


def _sample_idx_kernel(pt_ref, *refs, g_pages, n_steps, n_idx, dec_seq, topk, nbits, past, att_keys):
    del pt_ref
    pages = refs[:g_pages]
    qi_ref, wcol_ref, kinew_ref, bias_past_ref, bias_new_ref, kb_scr, key_scr, keyn_scr = refs[g_pages:]
    j = pl.program_id(1)
    pg = g_pages * PAGE_SIZE
    rows = 8
    row_id = lax.broadcasted_iota(I32, (rows, 1), 0)

    def scores_of(kmat):
        d = lax.dot_general(qi_ref[0], kmat, _NT, preferred_element_type=F32)
        wr = jnp.maximum(d, 0.0) * wcol_ref[0]
        s4 = jnp.sum(wr.reshape(dec_seq, n_idx, kmat.shape[0]), axis=1)
        return jnp.concatenate([s4, jnp.zeros((rows - dec_seq, kmat.shape[0]), F32)], axis=0)

    for r in range(g_pages):
        kb_scr[r * PAGE_SIZE:(r + 1) * PAGE_SIZE, :] = pages[r][0].astype(BF16)
    key_scr[j] = jnp.where(row_id < dec_seq, _sortable_key(scores_of(kb_scr[...])), INT_MIN)

    @pl.when(j == n_steps - 1)
    def _():
        sn = scores_of(kinew_ref[0])
        coln = lax.broadcasted_iota(I32, (1, LANES), 1)
        okn = jnp.logical_and(jnp.logical_and(coln < dec_seq, coln <= row_id), row_id < dec_seq)
        keyn_scr[...] = jnp.where(okn, _sortable_key(sn), INT_MIN)

        def count(pred):
            cnt = jnp.zeros((rows, LANES), F32)
            for c in range(n_steps):
                p = pred(key_scr[c], c * pg + lax.broadcasted_iota(I32, (1, pg), 1))
                for jj in range(pg // LANES):
                    cnt = cnt + jnp.where(p[:, jj * LANES:(jj + 1) * LANES], 1.0, 0.0)
            cnt = cnt + jnp.where(pred(keyn_scr[...], past + coln), 1.0, 0.0)
            return jnp.sum(cnt, axis=-1, keepdims=True)

        thr, n_ge = _kth_largest_key(lambda cand: count(lambda k, col: k >= cand), (rows, 1), topk)
        n_gt = count(lambda k, col: k > thr)
        tie = jnp.logical_and(n_ge > topk, thr > INT_MIN)
        thr_valid = jnp.maximum(thr, INT_MIN + 1)
        need = topk - n_gt
        cut = _tie_index_cut(
            lambda cand: count(lambda k, col: jnp.logical_and(k == thr, col < cand)), need, (rows, 1), nbits)
        cut = jnp.where(tie, cut, jnp.int32(2 ** 30))

        def sel(k, col):
            return jnp.logical_and(k >= thr_valid, jnp.logical_or(k > thr, col <= cut))

        n_sub = pg // att_keys
        for c in range(n_steps):
            b = jnp.where(sel(key_scr[c], c * pg + lax.broadcasted_iota(I32, (1, pg), 1)), 0.0, NEG)
            for u in range(n_sub):
                bias_past_ref[0, n_sub * c + u] = b[:, u * att_keys:(u + 1) * att_keys]
        bias_new_ref[0] = jnp.where(sel(keyn_scr[...], past + coln), 0.0, NEG)


def _sample_select(page_table, cache_kidx, qi_rows, wcol, kinew, *, n_idx, dec_seq, name):
    bd, n_pages = page_table.shape
    g_pages = _pick(n_pages, (16, 8, 4, 2, 1))
    n_steps = n_pages // g_pages
    pg = g_pages * PAGE_SIZE
    att_keys = _pick(pg, (4 * PAGE_SIZE, 2 * PAGE_SIZE, PAGE_SIZE))
    n_att = n_steps * (pg // att_keys)
    past = n_pages * PAGE_SIZE
    topk = min(TOPK_MAX, (past + dec_seq) // 4)
    nbits = max(1, int(np.ceil(np.log2(past + LANES))))
    rq = dec_seq * n_idx

    def page_spec(r):
        return pl.BlockSpec((1, PAGE_SIZE, LANES), lambda b, j, pt: (pt[b, j * g_pages + r], 0, 0))

    grid_spec = pltpu.PrefetchScalarGridSpec(
        num_scalar_prefetch=1,
        grid=(bd, n_steps),
        in_specs=[page_spec(r) for r in range(g_pages)] + [
            pl.BlockSpec((1, rq, LANES), lambda b, j, pt: (b, 0, 0)),
            pl.BlockSpec((1, rq, 1), lambda b, j, pt: (b, 0, 0)),
            pl.BlockSpec((1, LANES, LANES), lambda b, j, pt: (b, 0, 0))],
        out_specs=[pl.BlockSpec((1, n_att, 8, att_keys), lambda b, j, pt: (b, 0, 0, 0)),
                   pl.BlockSpec((1, 8, LANES), lambda b, j, pt: (b, 0, 0))],
        scratch_shapes=[pltpu.VMEM((pg, LANES), BF16), pltpu.VMEM((n_steps, 8, pg), I32),
                        pltpu.VMEM((8, LANES), I32)],
    )
    return pl.pallas_call(
        functools.partial(_sample_idx_kernel, g_pages=g_pages, n_steps=n_steps, n_idx=n_idx,
                          dec_seq=dec_seq, topk=topk, nbits=nbits, past=past, att_keys=att_keys),
        grid_spec=grid_spec,
        out_shape=[jax.ShapeDtypeStruct((bd, n_att, 8, att_keys), F32),
                   jax.ShapeDtypeStruct((bd, 8, LANES), F32)],
        compiler_params=_cparams(("parallel", "arbitrary")),
        name=name,
    )(page_table, *([cache_kidx] * g_pages), qi_rows, wcol, kinew)


def _sample_attn_kernel(pt_ref, *refs, g_pages, n_steps, n_heads, scale):
    del pt_ref
    kpages = refs[:g_pages]
    vpages = refs[g_pages:2 * g_pages]
    (qblk_ref, bias_ref, biasn_ref, knew_ref, vnew_ref, o_ref,
     kb_scr, vb_scr, m_scr, l_scr, acc_scr) = refs[2 * g_pages:]
    j = pl.program_id(1)
    rows = 8 * n_heads

    @pl.when(j == 0)
    def _():
        m_scr[...] = jnp.full(m_scr.shape, NEG, F32)
        l_scr[...] = jnp.zeros(l_scr.shape, F32)
        acc_scr[...] = jnp.zeros(acc_scr.shape, F32)

    def update(kb, vb, bias8):
        st = jnp.dot(kb, qblk_ref[0], preferred_element_type=F32)
        s = st.T * scale + jnp.concatenate([bias8] * n_heads, axis=0)
        m = m_scr[...]
        m_new = jnp.maximum(m, jnp.max(s, axis=-1, keepdims=True))
        alpha = jnp.exp2(m - m_new)
        p = jnp.exp2(s - m_new)
        l_scr[...] = alpha * l_scr[...] + jnp.sum(p, axis=-1, keepdims=True)
        acc_scr[...] = alpha * acc_scr[...] + jnp.dot(p.astype(BF16), vb, preferred_element_type=F32)
        m_scr[...] = m_new

    kg = 16
    for r in range(g_pages):
        for src, dst in ((kpages[r], kb_scr), (vpages[r], vb_scr)):
            for g in range(PAGE_SIZE // kg):
                blk = src[0, 0, g * kg * n_heads:(g + 1) * kg * n_heads, :].astype(BF16)
                by_head = pltpu.einshape("khd->hkd", blk.reshape(kg, n_heads, LANES))
                ks = slice(r * PAGE_SIZE + g * kg, r * PAGE_SIZE + (g + 1) * kg)
                for h in range(n_heads):
                    dst[ks, h * LANES:(h + 1) * LANES] = by_head[h]
    update(kb_scr[...], vb_scr[...], bias_ref[0, 0])

    @pl.when(j == n_steps - 1)
    def _():
        update(knew_ref[0], vnew_ref[0], biasn_ref[0])
        for h in range(n_heads):
            rs = slice(h * 8, (h + 1) * 8)
            cs = slice(h * LANES, (h + 1) * LANES)
            o_ref[0, :, cs] = acc_scr[rs, cs] / l_scr[rs, :]


def _sample_attention(page_table, cache_k, cache_v, layer, qblk, bias_past, bias_new, knew, vnew, *, n_heads, name):
    bd, n_pages = page_table.shape
    d_b = n_heads * LANES
    n_steps = bias_past.shape[1]
    g_pages = n_pages // n_steps
    pg = g_pages * PAGE_SIZE
    rows = 8 * n_heads

    def page_spec(r):
        return pl.BlockSpec((1, 1, PAGE_SIZE * n_heads, LANES),
                            lambda b, j, pt: (layer, pt[b, j * g_pages + r], 0, 0))

    grid_spec = pltpu.PrefetchScalarGridSpec(
        num_scalar_prefetch=1,
        grid=(bd, n_steps),
        in_specs=[page_spec(r) for r in range(g_pages)] * 2 + [
            pl.BlockSpec((1, d_b, rows), lambda b, j, pt: (b, 0, 0)),
            pl.BlockSpec((1, 1, 8, pg), lambda b, j, pt: (b, j, 0, 0)),
            pl.BlockSpec((1, 8, LANES), lambda b, j, pt: (b, 0, 0)),
            pl.BlockSpec((1, LANES, d_b), lambda b, j, pt: (b, 0, 0)),
            pl.BlockSpec((1, LANES, d_b), lambda b, j, pt: (b, 0, 0))],
        out_specs=pl.BlockSpec((1, 8, d_b), lambda b, j, pt: (b, 0, 0)),
        scratch_shapes=[pltpu.VMEM((pg, d_b), BF16), pltpu.VMEM((pg, d_b), BF16),
                        pltpu.VMEM((rows, 1), F32), pltpu.VMEM((rows, 1), F32), pltpu.VMEM((rows, d_b), F32)],
    )
    return pl.pallas_call(
        functools.partial(_sample_attn_kernel, g_pages=g_pages, n_steps=n_steps, n_heads=n_heads,
                          scale=float(LANES ** -0.5 * np.log2(np.e))),
        grid_spec=grid_spec,
        out_shape=jax.ShapeDtypeStruct((bd, 8, d_b), F32),
        compiler_params=_cparams(("parallel", "arbitrary")),
        name=name,
    )(page_table, *([cache_k] * g_pages), *([cache_v] * g_pages), qblk, bias_past, bias_new, knew, vnew)


def _merge_kernel(h_ref, a_ref, attn_ref, zs_ref, wga_ref, wgb_ref, wua_ref, wub_ref, o_ref, bg_scr):
    @pl.when(pl.program_id(1) == 0)
    def _():
        bg_scr[...] = (attn_ref[...].astype(F32) * zs_ref[...].astype(F32)).astype(BF16)

    h = h_ref[...]
    ga = jnp.dot(h, wga_ref[...], preferred_element_type=F32)
    gb = jnp.dot(h, wgb_ref[...], preferred_element_type=F32)
    a_up = jnp.dot(a_ref[...], wua_ref[...], preferred_element_type=F32)
    b_up = jnp.dot(bg_scr[...], wub_ref[...], preferred_element_type=F32)
    o_ref[...] = (_sigmoid(ga) * a_up + _sigmoid(gb) * b_up).astype(o_ref.dtype)


def _merge(h, a_gated, attn, zs, w_ga, w_gb, w_ua, w_ub, *, name):
    m, d = h.shape
    d_a, d_b = a_gated.shape[1], attn.shape[1]
    tm = _pick(m, (512, 256, 128))
    tn = _pick(d, (256, 128))
    row = lambda i, j: (i, 0)
    col = lambda i, j: (0, j)
    return pl.pallas_call(
        _merge_kernel,
        grid=(m // tm, d // tn),
        in_specs=[pl.BlockSpec((tm, d), row), pl.BlockSpec((tm, d_a), row),
                  pl.BlockSpec((tm, d_b), row), pl.BlockSpec((tm, d_b), row),
                  pl.BlockSpec((d, tn), col), pl.BlockSpec((d, tn), col),
                  pl.BlockSpec((d_a, tn), col), pl.BlockSpec((d_b, tn), col)],
        out_specs=pl.BlockSpec((tm, tn), lambda i, j: (i, j)),
        out_shape=jax.ShapeDtypeStruct((m, d), BF16),
        scratch_shapes=[pltpu.VMEM((tm, d_b), BF16)],
        compiler_params=_cparams(("parallel", "arbitrary")),
        name=name,
    )(h, a_gated, attn, zs, w_ga, w_gb, w_ua, w_ub)


def _resid_kernel(mg_ref, w_ref, x_ref, o_ref):
    o_ref[...] = x_ref[...] + jnp.dot(mg_ref[...], w_ref[...], preferred_element_type=F32)


def _resid_proj(merged, w_o, x, *, name):
    m, d = x.shape
    tm = _pick(m, (1024, 512, 256, 128))
    tn = _pick(d, (512, 256, 128))
    return pl.pallas_call(
        _resid_kernel,
        grid=(m // tm, d // tn),
        in_specs=[pl.BlockSpec((tm, merged.shape[1]), lambda i, j: (i, 0)),
                  pl.BlockSpec((merged.shape[1], tn), lambda i, j: (0, j)),
                  pl.BlockSpec((tm, tn), lambda i, j: (i, j))],
        out_specs=pl.BlockSpec((tm, tn), lambda i, j: (i, j)),
        out_shape=jax.ShapeDtypeStruct((m, d), F32),
        compiler_params=_cparams(("parallel", "arbitrary")),
        name=name,
    )(merged, w_o, x)


def _ple_kernel(hn_ref, wg_ref, p_ref, wp_ref, x1_ref, o_ref):
    gate = _sigmoid(jnp.dot(hn_ref[...], wg_ref[...], preferred_element_type=F32))
    emb = jnp.dot(p_ref[...], wp_ref[...], preferred_element_type=F32)
    o_ref[...] = x1_ref[...] + gate * emb


def _ple(hn, w_gate, p, w_proj, x1, *, name):
    m, d = x1.shape
    tm = _pick(m, (1024, 512, 256, 128))
    tn = _pick(d, (512, 256, 128))
    return pl.pallas_call(
        _ple_kernel,
        grid=(m // tm, d // tn),
        in_specs=[pl.BlockSpec((tm, d), lambda i, j: (i, 0)),
                  pl.BlockSpec((d, tn), lambda i, j: (0, j)),
                  pl.BlockSpec((tm, p.shape[1]), lambda i, j: (i, 0)),
                  pl.BlockSpec((p.shape[1], tn), lambda i, j: (0, j)),
                  pl.BlockSpec((tm, tn), lambda i, j: (i, j))],
        out_specs=pl.BlockSpec((tm, tn), lambda i, j: (i, j)),
        out_shape=jax.ShapeDtypeStruct((m, d), F32),
        compiler_params=_cparams(("parallel", "arbitrary")),
        name=name,
    )(hn, w_gate, p, w_proj, x1)


def _rope_tables(pos):
    half = LANES // 2
    inv = ROPE_THETA ** (-jnp.arange(half, dtype=F32) / half)
    ang = pos.astype(F32)[:, None] * inv[None, :]
    cos, sin = jnp.cos(ang), jnp.sin(ang)
    return jnp.concatenate([cos, cos], axis=-1), jnp.concatenate([-sin, sin], axis=-1)


def _input_weights(w_in, d_model, d_a, d_b, n_idx):
    widths = [3 * d_a, d_b, d_b, d_b, d_b, n_idx * LANES, LANES + n_idx, d_model, d_model]
    offs = [int(v) for v in np.concatenate([[0], np.cumsum(widths)])]
    w_bf = w_in.astype(BF16)
    cols = dict(zip(("a", "q", "k", "v", "zb", "qi", "kiwi"), offs[:7]))
    return w_bf, cols, w_bf[:, offs[7]:offs[8]], w_bf[:, offs[8]:offs[9]]


def _layer(x, p, pos, tag, w, ws, bs, *, n_heads, n_idx, emit_vn, v_chunk=0):
    m, d_model = x.shape
    cosf, sinf = _rope_tables(pos)
    h = _rmsnorm(x, w["g_norm"], f"{tag}_norm")
    w_all, cols, d_b = w["w_all"], w["cols"], n_heads * LANES
    a_out = _branch_a(h, w_all, w["g_va"], w["b_va"], ws, bs, emit_vn=emit_vn, name=f"{tag}_branch_a")
    gq = jnp.tile(w["g_q"], n_heads).reshape(1, -1)
    gk = jnp.tile(w["g_k"], n_heads).reshape(1, -1)
    (q_bf,) = _proj(h, w_all, col_off=cols["q"], n=d_b, kind="norm_rope", n_f32=0, n_bf16=1, gain=gq, cos=cosf,
                    sin=sinf, name=f"{tag}_q")
    k_f32, k_bf = _proj(h, w_all, col_off=cols["k"], n=d_b, kind="norm_rope", n_f32=1, n_bf16=1, gain=gk, cos=cosf,
                        sin=sinf, name=f"{tag}_k")
    if v_chunk:
        v_f32, v_bf = _proj(h, w_all, col_off=cols["v"], n=d_b, kind="plain", n_f32=1, n_bf16=0, t_chunk=v_chunk,
                            name=f"{tag}_v")
    else:
        v_f32, v_bf = _proj(h, w_all, col_off=cols["v"], n=d_b, kind="plain", n_f32=1, n_bf16=1, name=f"{tag}_v")
    (zs,) = _proj(h, w_all, col_off=cols["zb"], n=d_b, kind="silu", n_f32=0, n_bf16=1, name=f"{tag}_zb")
    (qi,) = _proj(h, w_all, col_off=cols["qi"], n=n_idx * LANES, kind="rope", n_f32=0, n_bf16=1, cos=cosf, sin=sinf,
                  name=f"{tag}_qi")
    ki_f32, ki_bf, wi = _proj(h, w_all, col_off=cols["kiwi"], n=2 * LANES, kind="kiwi", n_f32=0, n_bf16=0, cos=cosf,
                              sin=sinf, scale=float(n_idx ** -0.5 * LANES ** -0.5), name=f"{tag}_kiwi")
    return dict(h=h, a=a_out, q=q_bf, k=k_f32, kb=k_bf, v=v_f32, vb=v_bf, zs=zs, qi=qi, ki=ki_f32, kib=ki_bf, wi=wi)


def _finish(x, p, t, attn, tag, w):
    merged = _merge(t["h"], t["a"][0], attn, t["zs"], w["w_ga"], w["w_gb"], w["w_ua"], w["w_ub"], name=f"{tag}_merge")
    x1 = _resid_proj(merged, w["w_o"], x, name=f"{tag}_resid")
    hn = _rmsnorm(x1, w["g_ple"], f"{tag}_ple_norm")
    return _ple(hn, w["w_pg"], p.astype(BF16), w["w_pp"], x1, name=f"{tag}_ple")


def kernel(x_prompt, x_sample, p_prompt, p_sample, cache_k, cache_v, cache_kidx, page_table, g_norm, w_in, g_va,
           b_va, w_s, b_s, g_q, g_k, w_up_a, w_up_b, w_o, g_ple, w_ple_gate, w_ple_proj):
    depth = w_in.shape[0]
    batch, seq, d_model = x_prompt.shape
    bd, dec_seq, _ = x_sample.shape
    n_phys, _, n_heads, head_dim = cache_k.shape[1:]
    n_groups = w_s.shape[1]
    d_a, d_b = w_up_a.shape[1], w_up_b.shape[1]
    n_idx = w_in.shape[2] - (3 * d_a + 4 * d_b + LANES + 2 * d_model)
    n_idx = n_idx // (LANES + 1)
    n_pages = page_table.shape[1]
    past = n_pages * PAGE_SIZE
    assert batch == 1 and head_dim == LANES and cache_kidx.shape[-1] == LANES and d_a == n_groups * LANES
    assert seq % CHUNK == 0 and bd * dec_seq == CHUNK and dec_seq <= 8 and d_b == n_heads * LANES
    ms = bd * dec_seq

    yp = x_prompt.reshape(seq, d_model)
    ys = x_sample.reshape(ms, d_model)
    pos_p = jnp.arange(seq)
    pos_s = jnp.tile(past + jnp.arange(dec_seq), bd)
    outs = {k: [] for k in ("kp", "vp", "kip", "ks", "vs", "kis", "vas")}
    tril = jnp.tril(jnp.ones((CHUNK, CHUNK), bool))
    cache_k2 = cache_k.reshape(depth, n_phys, PAGE_SIZE * n_heads, LANES)
    cache_v2 = cache_v.reshape(depth, n_phys, PAGE_SIZE * n_heads, LANES)
    for i in range(depth):
        w_all, cols, w_ga, w_gb = _input_weights(w_in[i], d_model, d_a, d_b, n_idx)
        w = dict(w_all=w_all, cols=cols, w_ga=w_ga, w_gb=w_gb)
        w.update(g_norm=g_norm[i], g_va=g_va[i], b_va=b_va[i], g_q=g_q[i], g_k=g_k[i], g_ple=g_ple[i],
                 w_ua=w_up_a[i].astype(BF16), w_ub=w_up_b[i].astype(BF16), w_o=w_o[i].astype(BF16),
                 w_pg=w_ple_gate[i].astype(BF16), w_pp=w_ple_proj[i].astype(BF16))
        ws_p = jnp.where(tril[None], w_s[i], 0.0).astype(BF16)
        bs_p = jnp.broadcast_to(b_s[i][:, :, None], (n_groups, CHUNK, LANES))
        small = jnp.where(tril[None, :dec_seq, :dec_seq], w_s[i][:, :dec_seq, :dec_seq], 0.0)
        ws_s = jnp.einsum("ab,gts->gatbs", jnp.eye(bd, dtype=F32), small).reshape(n_groups, ms, ms).astype(BF16)
        bs_s = jnp.broadcast_to(jnp.tile(b_s[i][:, :dec_seq], (1, bd))[:, :, None], (n_groups, ms, LANES))

        tp = _layer(yp, p_prompt[i, 0], pos_p, "prompt", w, ws_p, bs_p, n_heads=n_heads, n_idx=n_idx, emit_vn=False,
                    v_chunk=_pick(seq, (512, 256, 128)))
        mask = _prompt_select(tp["qi"], tp["wi"], tp["kib"], n_idx=n_idx, name="prompt_select")
        attn_p = _prompt_attention(tp["q"], tp["kb"], tp["vb"], mask, name="prompt_attention")
        yp_new = _finish(yp, p_prompt[i, 0], tp, attn_p, "prompt", w)

        tsm = _layer(ys, p_sample[i].reshape(ms, -1), pos_s, "sample", w, ws_s, bs_s, n_heads=n_heads, n_idx=n_idx,
                     emit_vn=True)
        qi_rows = tsm["qi"].reshape(bd, dec_seq * n_idx, LANES)
        wcol = tsm["wi"][:, :n_idx].reshape(bd, dec_seq * n_idx, 1)
        kinew = jnp.pad(tsm["kib"].reshape(bd, dec_seq, LANES), ((0, 0), (0, LANES - dec_seq), (0, 0)))
        bias_past, bias_new = _sample_select(page_table, cache_kidx[i], qi_rows, wcol, kinew, n_idx=n_idx,
                                             dec_seq=dec_seq, name="sample_select")
        q4 = jnp.pad(tsm["q"].reshape(bd, dec_seq, n_heads, LANES), ((0, 0), (0, 8 - dec_seq), (0, 0), (0, 0)))
        qblk = jnp.einsum("bthd,hg->bhdgt", q4, jnp.eye(n_heads, dtype=BF16)).reshape(bd, d_b, n_heads * 8)
        padrows = ((0, 0), (0, LANES - dec_seq), (0, 0))
        knew = jnp.pad(tsm["kb"].reshape(bd, dec_seq, d_b), padrows)
        vnew = jnp.pad(tsm["vb"].reshape(bd, dec_seq, d_b), padrows)
        attn_s8 = _sample_attention(page_table, cache_k2, cache_v2, i, qblk, bias_past, bias_new,
                                    knew, vnew, n_heads=n_heads, name="sample_attention")
        attn_s = attn_s8[:, :dec_seq].reshape(ms, d_b).astype(BF16)
        ys_new = _finish(ys, p_sample[i].reshape(ms, -1), tsm, attn_s, "sample", w)

        outs["kp"].append(tp["k"].reshape(batch, seq, n_heads, LANES))
        outs["vp"].append(tp["v"].reshape(batch, seq, n_heads, LANES))
        outs["kip"].append(tp["ki"].reshape(batch, seq, LANES))
        outs["ks"].append(tsm["k"].reshape(bd, dec_seq, n_heads, LANES))
        outs["vs"].append(tsm["v"].reshape(bd, dec_seq, n_heads, LANES))
        outs["kis"].append(tsm["ki"].reshape(bd, dec_seq, LANES))
        outs["vas"].append(tsm["a"][1].reshape(bd, dec_seq, d_a))
        yp, ys = yp_new, ys_new

    return (yp.reshape(batch, seq, d_model), ys.reshape(bd, dec_seq, d_model),
            jnp.stack(outs["kp"]), jnp.stack(outs["vp"]), jnp.stack(outs["kip"]),
            jnp.stack(outs["ks"]), jnp.stack(outs["vs"]), jnp.stack(outs["kis"]), jnp.stack(outs["vas"]))
```

```python
import functools

import jax
import jax.numpy as jnp
import numpy as np
from jax import lax
from jax.experimental import pallas as pl
from jax.experimental.pallas import tpu as pltpu

F32 = jnp.float32
BF16 = jnp.bfloat16
I32 = jnp.int32

LANES = 128
PAGE_SIZE = 128
CHUNK = 128
TOPK_MAX = 256
ROPE_THETA = 10000.0
EPS = 1e-6
NEG = -1e30
INT_MIN = -(2 ** 31)
VMEM_LIMIT = 56 * 1024 * 1024

_NT = (((1,), (1,)), ((), ()))


def _cparams(sem):
    return pltpu.CompilerParams(dimension_semantics=sem, vmem_limit_bytes=VMEM_LIMIT)


def _sigmoid(x):
    return 1.0 / (1.0 + jnp.exp(-x))


def _pick(n, cands):
    for c in cands:
        if n % c == 0:
            return c
    return n


def _rmsnorm_kernel(x_ref, g_ref, o_ref):
    x = x_ref[...]
    ms = jnp.mean(x * x, axis=-1, keepdims=True)
    o_ref[...] = (x * lax.rsqrt(ms + EPS) * g_ref[...]).astype(o_ref.dtype)


def _rmsnorm(x, g, name):
    m, d = x.shape
    tm = _pick(m, (256, 128))
    return pl.pallas_call(
        _rmsnorm_kernel,
        grid=(m // tm,),
        in_specs=[pl.BlockSpec((tm, d), lambda i: (i, 0)),
                  pl.BlockSpec((1, d), lambda i: (0, 0))],
        out_specs=pl.BlockSpec((tm, d), lambda i: (i, 0)),
        out_shape=jax.ShapeDtypeStruct((m, d), BF16),
        compiler_params=_cparams(("parallel",)),
        name=name,
    )(x, g.reshape(1, d))


def _rope(x, cos, sin_signed):
    return x * cos + pltpu.roll(x, LANES // 2, axis=1) * sin_signed


def _proj_kernel(*refs, kind, n_f32, n_bf16, tn, scale, t_chunk):
    h_ref, w_ref = refs[0], refs[1]
    acc = jnp.dot(h_ref[...], w_ref[...], preferred_element_type=F32)
    if t_chunk:
        t_ref, refs = refs[-1], refs[:-1]
        for a in range(acc.shape[0] // t_chunk):
            t_ref[a] = acc[a * t_chunk:(a + 1) * t_chunk, :].T.astype(BF16)
    pos = 2
    if kind == "norm_rope":
        g_ref, cos_ref, sin_ref = refs[pos:pos + 3]
        pos += 3
    elif kind in ("rope", "kiwi"):
        cos_ref, sin_ref = refs[pos:pos + 2]
        pos += 2
    outs = refs[pos:]
    if kind == "kiwi":
        ki = _rope(acc[:, :LANES], cos_ref[...], sin_ref[...])
        outs[0][...] = ki
        outs[1][...] = ki.astype(BF16)
        outs[2][...] = acc[:, LANES:] * scale
        return
    for j in range(tn // LANES):
        sl = slice(j * LANES, (j + 1) * LANES)
        x = acc[:, sl]
        if kind == "norm_rope":
            ms = jnp.mean(x * x, axis=-1, keepdims=True)
            x = x * lax.rsqrt(ms + EPS) * g_ref[:, sl]
            x = _rope(x, cos_ref[...], sin_ref[...])
        elif kind == "rope":
            x = _rope(x, cos_ref[...], sin_ref[...])
        elif kind == "silu":
            x = x * _sigmoid(x)
        k = 0
        for _ in range(n_f32):
            outs[k][:, sl] = x
            k += 1
        for _ in range(n_bf16):
            outs[k][:, sl] = x.astype(BF16)
            k += 1


def _proj(h, w, *, kind, n_f32, n_bf16, name, col_off, n, gain=None, cos=None, sin=None, scale=1.0, t_chunk=0):
    m, kdim = h.shape
    tm = _pick(m, (1024, 512, 256, 128))
    assert t_chunk == 0 or tm % t_chunk == 0
    tn = 256 if kind == "kiwi" else _pick(np.gcd(n, col_off) if col_off else n, (512, 256, 128))
    assert col_off % tn == 0 and n % tn == 0
    blk_off = col_off // tn
    in_specs = [pl.BlockSpec((tm, kdim), lambda i, j: (i, 0)),
                pl.BlockSpec((kdim, tn), lambda i, j: (0, j + blk_off))]
    args = [h, w]
    if kind == "norm_rope":
        in_specs.append(pl.BlockSpec((1, tn), lambda i, j: (0, j)))
        args.append(gain)
    if kind in ("norm_rope", "rope", "kiwi"):
        in_specs += [pl.BlockSpec((tm, LANES), lambda i, j: (i, 0))] * 2
        args += [cos, sin]
    if kind == "kiwi":
        out_shape = [jax.ShapeDtypeStruct((m, LANES), F32), jax.ShapeDtypeStruct((m, LANES), BF16),
                     jax.ShapeDtypeStruct((m, LANES), F32)]
        out_specs = [pl.BlockSpec((tm, LANES), lambda i, j: (i, 0))] * 3
    else:
        out_shape = ([jax.ShapeDtypeStruct((m, n), F32)] * n_f32 + [jax.ShapeDtypeStruct((m, n), BF16)] * n_bf16)
        out_specs = [pl.BlockSpec((tm, tn), lambda i, j: (i, j))] * (n_f32 + n_bf16)
    if t_chunk:
        out_shape = out_shape + [jax.ShapeDtypeStruct((m // t_chunk, n, t_chunk), BF16)]
        out_specs = out_specs + [pl.BlockSpec((tm // t_chunk, tn, t_chunk), lambda i, j: (i, j, 0))]
    return pl.pallas_call(
        functools.partial(_proj_kernel, kind=kind, n_f32=n_f32, n_bf16=n_bf16, tn=tn, scale=scale,
                          t_chunk=t_chunk),
        grid=(m // tm, n // tn),
        in_specs=in_specs,
        out_specs=out_specs,
        out_shape=out_shape,
        compiler_params=_cparams(("parallel", "arbitrary")),
        name=name,
    )(*args)


def _branch_a_kernel(h_ref, w_ref, gva_ref, bva_ref, ws_ref, bs_ref, *rest, tm, tn, d_a, emit_vn):
    if emit_vn:
        a_ref, vn_ref, scr = rest
    else:
        a_ref, scr = rest
    n = pl.program_id(1)
    nseg = d_a // tn
    per = tn // LANES
    scr[n] = jnp.dot(h_ref[...], w_ref[...], preferred_element_type=F32)

    @pl.when(n == 3 * nseg - 1)
    def _():
        s1 = jnp.zeros((tm, 1), F32)
        for j in range(nseg):
            s1 = s1 + jnp.sum(scr[nseg + j], axis=-1, keepdims=True)
        mu = s1 / d_a
        s2 = jnp.zeros((tm, 1), F32)
        for j in range(nseg):
            dlt = scr[nseg + j] - mu
            s2 = s2 + jnp.sum(dlt * dlt, axis=-1, keepdims=True)
        rstd = lax.rsqrt(s2 / d_a + EPS)
        for g in range(d_a // LANES):
            t_idx, sl = g // per, slice((g % per) * LANES, (g % per + 1) * LANES)
            gsl = slice(g * LANES, (g + 1) * LANES)
            vn = (scr[nseg + t_idx][:, sl] - mu) * rstd * gva_ref[:, gsl] + bva_ref[:, gsl]
            if emit_vn:
                vn_ref[:, gsl] = vn
            vb = vn.astype(BF16)
            u = scr[t_idx][:, sl]
            z = scr[2 * nseg + t_idx][:, sl]
            for c in range(tm // CHUNK):
                rs = slice(c * CHUNK, (c + 1) * CHUNK)
                mixed = jnp.dot(ws_ref[g], vb[rs], preferred_element_type=F32) + bs_ref[g]
                a_ref[rs, gsl] = (u[rs] * mixed * (z[rs] * _sigmoid(z[rs]))).astype(BF16)


def _branch_a(h, w_a, g_va, b_va, ws, bs, *, emit_vn, name):
    m, kdim = h.shape
    d_a = g_va.shape[0]
    tm = _pick(m, (512, 256, 128))
    tn = _pick(d_a, (512, 256, 128))
    ng = d_a // LANES
    out_shape = [jax.ShapeDtypeStruct((m, d_a), BF16)]
    out_specs = [pl.BlockSpec((tm, d_a), lambda i, j: (i, 0))]
    if emit_vn:
        out_shape.append(jax.ShapeDtypeStruct((m, d_a), F32))
        out_specs.append(pl.BlockSpec((tm, d_a), lambda i, j: (i, 0)))
    return pl.pallas_call(
        functools.partial(_branch_a_kernel, tm=tm, tn=tn, d_a=d_a, emit_vn=emit_vn),
        grid=(m // tm, 3 * d_a // tn),
        in_specs=[pl.BlockSpec((tm, kdim), lambda i, j: (i, 0)),
                  pl.BlockSpec((kdim, tn), lambda i, j: (0, j)),
                  pl.BlockSpec((1, d_a), lambda i, j: (0, 0)),
                  pl.BlockSpec((1, d_a), lambda i, j: (0, 0)),
                  pl.BlockSpec((ng, CHUNK, CHUNK), lambda i, j: (0, 0, 0)),
                  pl.BlockSpec((ng, CHUNK, LANES), lambda i, j: (0, 0, 0))],
        out_specs=out_specs,
        out_shape=out_shape,
        scratch_shapes=[pltpu.VMEM((3 * d_a // tn, tm, tn), F32)],
        compiler_params=_cparams(("parallel", "arbitrary")),
        name=name,
    )(h, w_a, g_va.reshape(1, d_a), b_va.reshape(1, d_a), ws, bs)


def _sortable_key(score):
    bits = lax.bitcast_convert_type(score, I32)
    return jnp.where(bits < 0, (bits ^ 0x7FFFFFFF) + 1, bits)


def _kth_largest_key(count_ge, shape, topk):
    n_valid = count_ge(jnp.full(shape, INT_MIN + 1, I32))

    def unresolved(cnt):
        return (jnp.max(jnp.where(jnp.logical_and(cnt != topk, n_valid >= topk), 1.0, 0.0)) > 0.0).astype(I32)

    def cond(state):
        b, _, _, go = state
        return jnp.logical_and(b < 32, go > 0)

    def body(state):
        b, lo, cnt_lo, _ = state
        cand = lo + lax.shift_left(jnp.int32(1), jnp.int32(31) - b)
        cnt = count_ge(cand)
        take = cnt >= topk
        lo = jnp.where(take, cand, lo)
        cnt_lo = jnp.where(take, cnt, cnt_lo)
        return b + 1, lo, cnt_lo, unresolved(cnt_lo)

    lo0 = jnp.full(shape, INT_MIN, I32)
    cnt0 = jnp.full(shape, 2.0 ** 31, F32)
    _, thr, n_ge, _ = lax.while_loop(cond, body, (jnp.int32(0), lo0, cnt0, jnp.int32(1)))
    return thr, n_ge


def _tie_index_cut(count_eq_below, need, shape, nbits):
    def body(b, ans):
        cand = ans + lax.shift_left(jnp.int32(1), jnp.int32(nbits - 1) - b)
        return jnp.where(count_eq_below(cand) < need, cand, ans)
    return lax.fori_loop(0, nbits, body, jnp.zeros(shape, I32))


def _idx_kernel(qi_ref, wi_ref, ki_ref, mask_ref, key_scr, wb_scr, *, tq, kc, n_idx, n_chunks_total, topk, nbits):
    i = pl.program_id(0)
    nch = i + 1
    per = kc // LANES
    for h in range(n_idx):
        wb_scr[h] = jnp.broadcast_to(wi_ref[:, h:h + 1], (tq, LANES))
    q_pos = i * tq + lax.broadcasted_iota(I32, (1, tq), 1)
    sub = lax.broadcasted_iota(I32, (kc, 1), 0)

    def score_chunk(c, carry):
        kic = ki_ref[pl.ds(pl.multiple_of(c * kc, kc), kc), :]
        acc = [jnp.zeros((tq, LANES), F32) for _ in range(per)]
        for h in range(n_idx):
            d = lax.dot_general(qi_ref[:, h * LANES:(h + 1) * LANES], kic, _NT, preferred_element_type=F32)
            w = wb_scr[h]
            for j in range(per):
                acc[j] = acc[j] + jnp.maximum(d[:, j * LANES:(j + 1) * LANES], 0.0) * w
        score_t = jnp.concatenate(acc, axis=1).T
        key_scr[c] = jnp.where(c * kc + sub <= q_pos, _sortable_key(score_t), INT_MIN)
        return carry

    lax.fori_loop(0, nch, score_chunk, 0)

    def count(pred):
        def body(c, cnt):
            hit = jnp.where(pred(key_scr[c], c * kc + sub), 1.0, 0.0)
            return cnt + jnp.sum(hit.reshape(kc // 8, 8, tq), axis=0)
        cnt = lax.fori_loop(0, nch, body, jnp.zeros((8, tq), F32))
        return jnp.sum(cnt, axis=0, keepdims=True)

    def write_mask(pred):
        def body(c, carry):
            mask_ref[0, c] = jnp.where(pred(key_scr[c], c * kc + sub), 1.0, 0.0).astype(BF16)
            return carry
        lax.fori_loop(0, nch, body, 0)

    thr, n_ge = _kth_largest_key(lambda cand: count(lambda k, kidx: k >= cand), (1, tq), topk)
    tie = jnp.logical_and(n_ge > topk, thr > INT_MIN)
    thr_valid = jnp.maximum(thr, INT_MIN + 1)
    has_tie = jnp.max(jnp.where(tie, 1.0, 0.0)) > 0.0

    def write_fast():
        write_mask(lambda k, kidx: k >= thr_valid)

    def write_ties():
        n_gt = count(lambda k, kidx: k > thr)
        need = topk - n_gt
        cut = _tie_index_cut(lambda cand: count(lambda k, kidx: jnp.logical_and(k == thr, kidx < cand)),
                             need, (1, tq), nbits)
        cut = jnp.where(tie, cut, jnp.int32(2 ** 30))
        write_mask(lambda k, kidx: jnp.logical_and(k >= thr_valid, jnp.logical_or(k > thr, kidx <= cut)))

    lax.cond(has_tie, write_ties, write_fast)

    def fill(c, carry):
        mask_ref[0, c] = jnp.zeros((kc, tq), BF16)
        return carry
    lax.fori_loop(nch, n_chunks_total, fill, 0)


def _prompt_select(qi, wi, ki, *, n_idx, name):
    s = qi.shape[0]
    tq = kc = _pick(s, (256, 128))
    nq = s // tq
    topk = min(TOPK_MAX, s // 4)
    nbits = max(1, int(np.ceil(np.log2(s))))
    return pl.pallas_call(
        functools.partial(_idx_kernel, tq=tq, kc=kc, n_idx=n_idx, n_chunks_total=nq, topk=topk, nbits=nbits),
        grid=(nq,),
        in_specs=[pl.BlockSpec((tq, n_idx * LANES), lambda i: (i, 0)),
                  pl.BlockSpec((tq, LANES), lambda i: (i, 0)),
                  pl.BlockSpec((s, LANES), lambda i: (0, 0))],
        out_specs=pl.BlockSpec((1, nq, kc, tq), lambda i: (i, 0, 0, 0)),
        out_shape=jax.ShapeDtypeStruct((nq, nq, kc, tq), BF16),
        scratch_shapes=[pltpu.VMEM((nq, kc, tq), I32), pltpu.VMEM((n_idx, tq, LANES), F32)],
        compiler_params=_cparams(("parallel",)),
        name=name,
    )(qi, wi, ki)


def _attn_kernel(shift_ref, q_ref, k_ref, vt_ref, mask_ref, o_ref, m_scr, acc_scr, *, rq, rk, kc, hps, scale2,
                 fixed_shift):
    i = pl.program_id(1)
    acc_scr[...] = jnp.zeros(acc_scr.shape, F32)
    if not fixed_shift:
        m_scr[...] = jnp.full(m_scr.shape, NEG, F32)
    ones = jnp.ones((acc_scr.shape[1] - LANES, kc), BF16)
    shift = shift_ref[0]

    def body(c, carry):
        keep = jnp.concatenate(
            [jnp.concatenate([mask_ref[a, c * rk + b] for a in range(rq)], axis=1) for b in range(rk)], axis=0)
        rows = pl.ds(pl.multiple_of(c * kc, kc), kc)
        heads = [slice(hh * LANES, (hh + 1) * LANES) for hh in range(hps)]
        logits = [lax.dot_general(k_ref[rows, sl], q_ref[:, sl], _NT, preferred_element_type=F32) for sl in heads]
        for hh, sl in enumerate(heads):
            v_ext = jnp.concatenate([vt_ref[c, sl, :], ones], axis=0)
            if fixed_shift:
                p = jnp.exp2(logits[hh] * scale2 - shift).astype(BF16) * keep
                acc_scr[hh] = acc_scr[hh] + jnp.dot(v_ext, p, preferred_element_type=F32)
            else:
                s = logits[hh] * scale2 + (keep.astype(F32) - 1.0) * (-NEG)
                m = m_scr[hh]
                m_new = jnp.maximum(m, jnp.max(s, axis=0, keepdims=True))
                alpha = jnp.exp2(m - m_new)
                p = jnp.exp2(s - m_new).astype(BF16)
                acc_scr[hh] = alpha * acc_scr[hh] + jnp.dot(v_ext, p, preferred_element_type=F32)
                m_scr[hh] = m_new
        return carry

    lax.fori_loop(0, i + 1, body, 0)
    for hh in range(hps):
        acc = acc_scr[hh]
        o_ref[:, hh * LANES:(hh + 1) * LANES] = (acc[:LANES] / acc[LANES:LANES + 1]).T.astype(o_ref.dtype)


_FIXED_SHIFT_MAX = 50.0


def _prompt_attention(q, k, vt, mask, g_q, g_k, *, name):
    s, d_b = q.shape
    nqs, nks, kcs, tqs = mask.shape
    nkc, _, kc = vt.shape
    tq = kc
    rq, rk = tq // tqs, kc // kcs
    hps = _pick(d_b // LANES, (4, 2, 1))
    wblk = hps * LANES
    once = pl.Buffered(1)
    scale2 = float(LANES ** -0.5 * np.log2(np.e))
    bound = (1.01 * LANES * scale2) * jnp.max(jnp.abs(g_q)) * jnp.max(jnp.abs(g_k))

    def run(fixed_shift):
        return pl.pallas_call(
            functools.partial(_attn_kernel, rq=rq, rk=rk, kc=kc, hps=hps, scale2=scale2, fixed_shift=fixed_shift),
            grid=(d_b // wblk, s // tq),
            in_specs=[pl.BlockSpec(memory_space=pltpu.SMEM),
                      pl.BlockSpec((tq, wblk), lambda h, i: (i, h)),
                      pl.BlockSpec((s, wblk), lambda h, i: (0, h), pipeline_mode=once),
                      pl.BlockSpec((nkc, wblk, kc), lambda h, i: (0, h, 0), pipeline_mode=once),
                      pl.BlockSpec((rq, nks, kcs, tqs), lambda h, i: (i, 0, 0, 0))],
            out_specs=pl.BlockSpec((tq, wblk), lambda h, i: (i, h)),
            out_shape=jax.ShapeDtypeStruct((s, d_b), BF16),
            scratch_shapes=[pltpu.VMEM((hps, 1, tq), F32), pltpu.VMEM((hps, LANES + 16, tq), F32)],
            compiler_params=_cparams(("parallel", "arbitrary")),
            name=name + ("_fixed_shift" if fixed_shift else "_online"),
        )(bound.reshape(1).astype(F32), q, k, vt, mask)

    return lax.cond(bound <= _FIXED_SHIFT_MAX, lambda: run(True), lambda: run(False))


def _sample_idx_kernel(pt_ref, *refs, g_pages, n_steps, n_idx, dec_seq, topk, nbits, past, att_keys):
    del pt_ref
    pages = refs[:g_pages]
    qi_ref, wcol_ref, kinew_ref, bias_past_ref, bias_new_ref, kb_scr, key_scr, keyn_scr = refs[g_pages:]
    j = pl.program_id(1)
    pg = g_pages * PAGE_SIZE
    rows = 8
    row_id = lax.broadcasted_iota(I32, (rows, 1), 0)

    def scores_of(kmat):
        d = lax.dot_general(qi_ref[0], kmat, _NT, preferred_element_type=F32)
        wr = jnp.maximum(d, 0.0) * wcol_ref[0]
        s4 = jnp.sum(wr.reshape(dec_seq, n_idx, kmat.shape[0]), axis=1)
        return jnp.concatenate([s4, jnp.zeros((rows - dec_seq, kmat.shape[0]), F32)], axis=0)

    for r in range(g_pages):
        kb_scr[r * PAGE_SIZE:(r + 1) * PAGE_SIZE, :] = pages[r][0].astype(BF16)
    key_scr[j] = jnp.where(row_id < dec_seq, _sortable_key(scores_of(kb_scr[...])), INT_MIN)

    @pl.when(j == n_steps - 1)
    def _():
        sn = scores_of(kinew_ref[0])
        coln = lax.broadcasted_iota(I32, (1, LANES), 1)
        okn = jnp.logical_and(jnp.logical_and(coln < dec_seq, coln <= row_id), row_id < dec_seq)
        keyn_scr[...] = jnp.where(okn, _sortable_key(sn), INT_MIN)

        def count(pred):
            cnt = jnp.zeros((rows, LANES), F32)
            for c in range(n_steps):
                p = pred(key_scr[c], c * pg + lax.broadcasted_iota(I32, (1, pg), 1))
                for jj in range(pg // LANES):
                    cnt = cnt + jnp.where(p[:, jj * LANES:(jj + 1) * LANES], 1.0, 0.0)
            cnt = cnt + jnp.where(pred(keyn_scr[...], past + coln), 1.0, 0.0)
            return jnp.sum(cnt, axis=-1, keepdims=True)

        thr, n_ge = _kth_largest_key(lambda cand: count(lambda k, col: k >= cand), (rows, 1), topk)
        n_gt = count(lambda k, col: k > thr)
        tie = jnp.logical_and(n_ge > topk, thr > INT_MIN)
        thr_valid = jnp.maximum(thr, INT_MIN + 1)
        need = topk - n_gt
        cut = _tie_index_cut(
            lambda cand: count(lambda k, col: jnp.logical_and(k == thr, col < cand)), need, (rows, 1), nbits)
        cut = jnp.where(tie, cut, jnp.int32(2 ** 30))

        def sel(k, col):
            return jnp.logical_and(k >= thr_valid, jnp.logical_or(k > thr, col <= cut))

        n_sub = pg // att_keys
        for c in range(n_steps):
            b = jnp.where(sel(key_scr[c], c * pg + lax.broadcasted_iota(I32, (1, pg), 1)), 0.0, NEG)
            for u in range(n_sub):
                bias_past_ref[0, n_sub * c + u] = b[:, u * att_keys:(u + 1) * att_keys]
        bias_new_ref[0] = jnp.where(sel(keyn_scr[...], past + coln), 0.0, NEG)


def _sample_select(page_table, cache_kidx, qi_rows, wcol, kinew, *, n_idx, dec_seq, name):
    bd, n_pages = page_table.shape
    g_pages = _pick(n_pages, (16, 8, 4, 2, 1))
    n_steps = n_pages // g_pages
    pg = g_pages * PAGE_SIZE
    att_keys = _pick(pg, (4 * PAGE_SIZE, 2 * PAGE_SIZE, PAGE_SIZE))
    n_att = n_steps * (pg // att_keys)
    past = n_pages * PAGE_SIZE
    topk = min(TOPK_MAX, (past + dec_seq) // 4)
    nbits = max(1, int(np.ceil(np.log2(past + LANES))))
    rq = dec_seq * n_idx

    def page_spec(r):
        return pl.BlockSpec((1, PAGE_SIZE, LANES), lambda b, j, pt: (pt[b, j * g_pages + r], 0, 0))

    grid_spec = pltpu.PrefetchScalarGridSpec(
        num_scalar_prefetch=1,
        grid=(bd, n_steps),
        in_specs=[page_spec(r) for r in range(g_pages)] + [
            pl.BlockSpec((1, rq, LANES), lambda b, j, pt: (b, 0, 0)),
            pl.BlockSpec((1, rq, 1), lambda b, j, pt: (b, 0, 0)),
            pl.BlockSpec((1, LANES, LANES), lambda b, j, pt: (b, 0, 0))],
        out_specs=[pl.BlockSpec((1, n_att, 8, att_keys), lambda b, j, pt: (b, 0, 0, 0)),
                   pl.BlockSpec((1, 8, LANES), lambda b, j, pt: (b, 0, 0))],
        scratch_shapes=[pltpu.VMEM((pg, LANES), BF16), pltpu.VMEM((n_steps, 8, pg), I32),
                        pltpu.VMEM((8, LANES), I32)],
    )
    return pl.pallas_call(
        functools.partial(_sample_idx_kernel, g_pages=g_pages, n_steps=n_steps, n_idx=n_idx,
                          dec_seq=dec_seq, topk=topk, nbits=nbits, past=past, att_keys=att_keys),
        grid_spec=grid_spec,
        out_shape=[jax.ShapeDtypeStruct((bd, n_att, 8, att_keys), F32),
                   jax.ShapeDtypeStruct((bd, 8, LANES), F32)],
        compiler_params=_cparams(("parallel", "arbitrary")),
        name=name,
    )(page_table, *([cache_kidx] * g_pages), qi_rows, wcol, kinew)


def _sample_attn_kernel(pt_ref, *refs, g_pages, n_steps, n_heads, scale):
    del pt_ref
    kpages = refs[:g_pages]
    vpages = refs[g_pages:2 * g_pages]
    (qblk_ref, bias_ref, biasn_ref, knew_ref, vnew_ref, o_ref,
     kb_scr, vb_scr, m_scr, l_scr, acc_scr) = refs[2 * g_pages:]
    j = pl.program_id(1)
    rows = 8 * n_heads

    @pl.when(j == 0)
    def _():
        m_scr[...] = jnp.full(m_scr.shape, NEG, F32)
        l_scr[...] = jnp.zeros(l_scr.shape, F32)
        acc_scr[...] = jnp.zeros(acc_scr.shape, F32)

    def update(kb, vb, bias8):
        st = jnp.dot(kb, qblk_ref[0], preferred_element_type=F32)
        s = st.T * scale + jnp.concatenate([bias8] * n_heads, axis=0)
        m = m_scr[...]
        m_new = jnp.maximum(m, jnp.max(s, axis=-1, keepdims=True))
        alpha = jnp.exp2(m - m_new)
        p = jnp.exp2(s - m_new)
        l_scr[...] = alpha * l_scr[...] + jnp.sum(p, axis=-1, keepdims=True)
        acc_scr[...] = alpha * acc_scr[...] + jnp.dot(p.astype(BF16), vb, preferred_element_type=F32)
        m_scr[...] = m_new

    kg = 16
    for r in range(g_pages):
        for src, dst in ((kpages[r], kb_scr), (vpages[r], vb_scr)):
            for g in range(PAGE_SIZE // kg):
                blk = src[0, 0, g * kg * n_heads:(g + 1) * kg * n_heads, :].astype(BF16)
                by_head = pltpu.einshape("khd->hkd", blk.reshape(kg, n_heads, LANES))
                ks = slice(r * PAGE_SIZE + g * kg, r * PAGE_SIZE + (g + 1) * kg)
                for h in range(n_heads):
                    dst[ks, h * LANES:(h + 1) * LANES] = by_head[h]
    update(kb_scr[...], vb_scr[...], bias_ref[0, 0])

    @pl.when(j == n_steps - 1)
    def _():
        update(knew_ref[0], vnew_ref[0], biasn_ref[0])
        for h in range(n_heads):
            rs = slice(h * 8, (h + 1) * 8)
            cs = slice(h * LANES, (h + 1) * LANES)
            o_ref[0, :, cs] = acc_scr[rs, cs] / l_scr[rs, :]


def _sample_attention(page_table, cache_k, cache_v, layer, qblk, bias_past, bias_new, knew, vnew, *, n_heads, name):
    bd, n_pages = page_table.shape
    d_b = n_heads * LANES
    n_steps = bias_past.shape[1]
    g_pages = n_pages // n_steps
    pg = g_pages * PAGE_SIZE
    rows = 8 * n_heads

    def page_spec(r):
        return pl.BlockSpec((1, 1, PAGE_SIZE * n_heads, LANES),
                            lambda b, j, pt: (layer, pt[b, j * g_pages + r], 0, 0))

    grid_spec = pltpu.PrefetchScalarGridSpec(
        num_scalar_prefetch=1,
        grid=(bd, n_steps),
        in_specs=[page_spec(r) for r in range(g_pages)] * 2 + [
            pl.BlockSpec((1, d_b, rows), lambda b, j, pt: (b, 0, 0)),
            pl.BlockSpec((1, 1, 8, pg), lambda b, j, pt: (b, j, 0, 0)),
            pl.BlockSpec((1, 8, LANES), lambda b, j, pt: (b, 0, 0)),
            pl.BlockSpec((1, LANES, d_b), lambda b, j, pt: (b, 0, 0)),
            pl.BlockSpec((1, LANES, d_b), lambda b, j, pt: (b, 0, 0))],
        out_specs=pl.BlockSpec((1, 8, d_b), lambda b, j, pt: (b, 0, 0)),
        scratch_shapes=[pltpu.VMEM((pg, d_b), BF16), pltpu.VMEM((pg, d_b), BF16),
                        pltpu.VMEM((rows, 1), F32), pltpu.VMEM((rows, 1), F32), pltpu.VMEM((rows, d_b), F32)],
    )
    return pl.pallas_call(
        functools.partial(_sample_attn_kernel, g_pages=g_pages, n_steps=n_steps, n_heads=n_heads,
                          scale=float(LANES ** -0.5 * np.log2(np.e))),
        grid_spec=grid_spec,
        out_shape=jax.ShapeDtypeStruct((bd, 8, d_b), F32),
        compiler_params=_cparams(("parallel", "arbitrary")),
        name=name,
    )(page_table, *([cache_k] * g_pages), *([cache_v] * g_pages), qblk, bias_past, bias_new, knew, vnew)


def _merge_kernel(h_ref, a_ref, attn_ref, zs_ref, wga_ref, wgb_ref, wua_ref, wub_ref, o_ref, bg_scr):
    @pl.when(pl.program_id(1) == 0)
    def _():
        bg_scr[...] = (attn_ref[...].astype(F32) * zs_ref[...].astype(F32)).astype(BF16)

    h = h_ref[...]
    ga = jnp.dot(h, wga_ref[...], preferred_element_type=F32)
    gb = jnp.dot(h, wgb_ref[...], preferred_element_type=F32)
    a_up = jnp.dot(a_ref[...], wua_ref[...], preferred_element_type=F32)
    b_up = jnp.dot(bg_scr[...], wub_ref[...], preferred_element_type=F32)
    o_ref[...] = (_sigmoid(ga) * a_up + _sigmoid(gb) * b_up).astype(o_ref.dtype)


def _merge(h, a_gated, attn, zs, w_ga, w_gb, w_ua, w_ub, *, name):
    m, d = h.shape
    d_a, d_b = a_gated.shape[1], attn.shape[1]
    tm = _pick(m, (512, 256, 128))
    tn = _pick(d, (256, 128))
    row = lambda i, j: (i, 0)
    col = lambda i, j: (0, j)
    return pl.pallas_call(
        _merge_kernel,
        grid=(m // tm, d // tn),
        in_specs=[pl.BlockSpec((tm, d), row), pl.BlockSpec((tm, d_a), row),
                  pl.BlockSpec((tm, d_b), row), pl.BlockSpec((tm, d_b), row),
                  pl.BlockSpec((d, tn), col), pl.BlockSpec((d, tn), col),
                  pl.BlockSpec((d_a, tn), col), pl.BlockSpec((d_b, tn), col)],
        out_specs=pl.BlockSpec((tm, tn), lambda i, j: (i, j)),
        out_shape=jax.ShapeDtypeStruct((m, d), BF16),
        scratch_shapes=[pltpu.VMEM((tm, d_b), BF16)],
        compiler_params=_cparams(("parallel", "arbitrary")),
        name=name,
    )(h, a_gated, attn, zs, w_ga, w_gb, w_ua, w_ub)


def _resid_kernel(mg_ref, w_ref, x_ref, o_ref):
    o_ref[...] = x_ref[...] + jnp.dot(mg_ref[...], w_ref[...], preferred_element_type=F32)


def _resid_proj(merged, w_o, x, *, name):
    m, d = x.shape
    tm = _pick(m, (1024, 512, 256, 128))
    tn = _pick(d, (512, 256, 128))
    return pl.pallas_call(
        _resid_kernel,
        grid=(m // tm, d // tn),
        in_specs=[pl.BlockSpec((tm, merged.shape[1]), lambda i, j: (i, 0)),
                  pl.BlockSpec((merged.shape[1], tn), lambda i, j: (0, j)),
                  pl.BlockSpec((tm, tn), lambda i, j: (i, j))],
        out_specs=pl.BlockSpec((tm, tn), lambda i, j: (i, j)),
        out_shape=jax.ShapeDtypeStruct((m, d), F32),
        compiler_params=_cparams(("parallel", "arbitrary")),
        name=name,
    )(merged, w_o, x)


def _ple_kernel(hn_ref, wg_ref, p_ref, wp_ref, x1_ref, o_ref):
    gate = _sigmoid(jnp.dot(hn_ref[...], wg_ref[...], preferred_element_type=F32))
    emb = jnp.dot(p_ref[...], wp_ref[...], preferred_element_type=F32)
    o_ref[...] = x1_ref[...] + gate * emb


def _ple(hn, w_gate, p, w_proj, x1, *, name):
    m, d = x1.shape
    tm = _pick(m, (1024, 512, 256, 128))
    tn = _pick(d, (512, 256, 128))
    return pl.pallas_call(
        _ple_kernel,
        grid=(m // tm, d // tn),
        in_specs=[pl.BlockSpec((tm, d), lambda i, j: (i, 0)),
                  pl.BlockSpec((d, tn), lambda i, j: (0, j)),
                  pl.BlockSpec((tm, p.shape[1]), lambda i, j: (i, 0)),
                  pl.BlockSpec((p.shape[1], tn), lambda i, j: (0, j)),
                  pl.BlockSpec((tm, tn), lambda i, j: (i, j))],
        out_specs=pl.BlockSpec((tm, tn), lambda i, j: (i, j)),
        out_shape=jax.ShapeDtypeStruct((m, d), F32),
        compiler_params=_cparams(("parallel", "arbitrary")),
        name=name,
    )(hn, w_gate, p, w_proj, x1)


def _rope_tables(pos):
    half = LANES // 2
    inv = ROPE_THETA ** (-jnp.arange(half, dtype=F32) / half)
    ang = pos.astype(F32)[:, None] * inv[None, :]
    cos, sin = jnp.cos(ang), jnp.sin(ang)
    return jnp.concatenate([cos, cos], axis=-1), jnp.concatenate([-sin, sin], axis=-1)


def _input_weights(w_in, d_model, d_a, d_b, n_idx):
    widths = [3 * d_a, d_b, d_b, d_b, d_b, n_idx * LANES, LANES + n_idx, d_model, d_model]
    offs = [int(v) for v in np.concatenate([[0], np.cumsum(widths)])]
    w_bf = w_in.astype(BF16)
    cols = dict(zip(("a", "q", "k", "v", "zb", "qi", "kiwi"), offs[:7]))
    return w_bf, cols, w_bf[:, offs[7]:offs[8]], w_bf[:, offs[8]:offs[9]]


def _layer(x, p, pos, tag, w, ws, bs, *, n_heads, n_idx, emit_vn, v_chunk=0):
    m, d_model = x.shape
    cosf, sinf = _rope_tables(pos)
    h = _rmsnorm(x, w["g_norm"], f"{tag}_norm")
    w_all, cols, d_b = w["w_all"], w["cols"], n_heads * LANES
    a_out = _branch_a(h, w_all, w["g_va"], w["b_va"], ws, bs, emit_vn=emit_vn, name=f"{tag}_branch_a")
    gq = jnp.tile(w["g_q"], n_heads).reshape(1, -1)
    gk = jnp.tile(w["g_k"], n_heads).reshape(1, -1)
    (q_bf,) = _proj(h, w_all, col_off=cols["q"], n=d_b, kind="norm_rope", n_f32=0, n_bf16=1, gain=gq, cos=cosf,
                    sin=sinf, name=f"{tag}_q")
    k_f32, k_bf = _proj(h, w_all, col_off=cols["k"], n=d_b, kind="norm_rope", n_f32=1, n_bf16=1, gain=gk, cos=cosf,
                        sin=sinf, name=f"{tag}_k")
    if v_chunk:
        v_f32, v_bf = _proj(h, w_all, col_off=cols["v"], n=d_b, kind="plain", n_f32=1, n_bf16=0, t_chunk=v_chunk,
                            name=f"{tag}_v")
    else:
        v_f32, v_bf = _proj(h, w_all, col_off=cols["v"], n=d_b, kind="plain", n_f32=1, n_bf16=1, name=f"{tag}_v")
    (zs,) = _proj(h, w_all, col_off=cols["zb"], n=d_b, kind="silu", n_f32=0, n_bf16=1, name=f"{tag}_zb")
    (qi,) = _proj(h, w_all, col_off=cols["qi"], n=n_idx * LANES, kind="rope", n_f32=0, n_bf16=1, cos=cosf, sin=sinf,
                  name=f"{tag}_qi")
    ki_f32, ki_bf, wi = _proj(h, w_all, col_off=cols["kiwi"], n=2 * LANES, kind="kiwi", n_f32=0, n_bf16=0, cos=cosf,
                              sin=sinf, scale=float(n_idx ** -0.5 * LANES ** -0.5), name=f"{tag}_kiwi")
    return dict(h=h, a=a_out, q=q_bf, k=k_f32, kb=k_bf, v=v_f32, vb=v_bf, zs=zs, qi=qi, ki=ki_f32, kib=ki_bf, wi=wi)


def _finish(x, p, t, attn, tag, w):
    merged = _merge(t["h"], t["a"][0], attn, t["zs"], w["w_ga"], w["w_gb"], w["w_ua"], w["w_ub"], name=f"{tag}_merge")
    x1 = _resid_proj(merged, w["w_o"], x, name=f"{tag}_resid")
    hn = _rmsnorm(x1, w["g_ple"], f"{tag}_ple_norm")
    return _ple(hn, w["w_pg"], p.astype(BF16), w["w_pp"], x1, name=f"{tag}_ple")


def kernel(x_prompt, x_sample, p_prompt, p_sample, cache_k, cache_v, cache_kidx, page_table, g_norm, w_in, g_va,
           b_va, w_s, b_s, g_q, g_k, w_up_a, w_up_b, w_o, g_ple, w_ple_gate, w_ple_proj):
    depth = w_in.shape[0]
    batch, seq, d_model = x_prompt.shape
    bd, dec_seq, _ = x_sample.shape
    n_phys, _, n_heads, head_dim = cache_k.shape[1:]
    n_groups = w_s.shape[1]
    d_a, d_b = w_up_a.shape[1], w_up_b.shape[1]
    n_idx = w_in.shape[2] - (3 * d_a + 4 * d_b + LANES + 2 * d_model)
    n_idx = n_idx // (LANES + 1)
    n_pages = page_table.shape[1]
    past = n_pages * PAGE_SIZE
    assert batch == 1 and head_dim == LANES and cache_kidx.shape[-1] == LANES and d_a == n_groups * LANES
    assert seq % CHUNK == 0 and bd * dec_seq == CHUNK and dec_seq <= 8 and d_b == n_heads * LANES
    ms = bd * dec_seq

    yp = x_prompt.reshape(seq, d_model)
    ys = x_sample.reshape(ms, d_model)
    pos_p = jnp.arange(seq)
    pos_s = jnp.tile(past + jnp.arange(dec_seq), bd)
    outs = {k: [] for k in ("kp", "vp", "kip", "ks", "vs", "kis", "vas")}
    tril = jnp.tril(jnp.ones((CHUNK, CHUNK), bool))
    cache_k2 = cache_k.reshape(depth, n_phys, PAGE_SIZE * n_heads, LANES)
    cache_v2 = cache_v.reshape(depth, n_phys, PAGE_SIZE * n_heads, LANES)
    for i in range(depth):
        w_all, cols, w_ga, w_gb = _input_weights(w_in[i], d_model, d_a, d_b, n_idx)
        w = dict(w_all=w_all, cols=cols, w_ga=w_ga, w_gb=w_gb)
        w.update(g_norm=g_norm[i], g_va=g_va[i], b_va=b_va[i], g_q=g_q[i], g_k=g_k[i], g_ple=g_ple[i],
                 w_ua=w_up_a[i].astype(BF16), w_ub=w_up_b[i].astype(BF16), w_o=w_o[i].astype(BF16),
                 w_pg=w_ple_gate[i].astype(BF16), w_pp=w_ple_proj[i].astype(BF16))
        ws_p = jnp.where(tril[None], w_s[i], 0.0).astype(BF16)
        bs_p = jnp.broadcast_to(b_s[i][:, :, None], (n_groups, CHUNK, LANES))
        small = jnp.where(tril[None, :dec_seq, :dec_seq], w_s[i][:, :dec_seq, :dec_seq], 0.0)
        ws_s = jnp.einsum("ab,gts->gatbs", jnp.eye(bd, dtype=F32), small).reshape(n_groups, ms, ms).astype(BF16)
        bs_s = jnp.broadcast_to(jnp.tile(b_s[i][:, :dec_seq], (1, bd))[:, :, None], (n_groups, ms, LANES))

        tp = _layer(yp, p_prompt[i, 0], pos_p, "prompt", w, ws_p, bs_p, n_heads=n_heads, n_idx=n_idx, emit_vn=False,
                    v_chunk=_pick(seq, (512, 256, 128)))
        mask = _prompt_select(tp["qi"], tp["wi"], tp["kib"], n_idx=n_idx, name="prompt_select")
        attn_p = _prompt_attention(tp["q"], tp["kb"], tp["vb"], mask, g_q[i], g_k[i], name="prompt_attention")
        yp_new = _finish(yp, p_prompt[i, 0], tp, attn_p, "prompt", w)

        tsm = _layer(ys, p_sample[i].reshape(ms, -1), pos_s, "sample", w, ws_s, bs_s, n_heads=n_heads, n_idx=n_idx,
                     emit_vn=True)
        qi_rows = tsm["qi"].reshape(bd, dec_seq * n_idx, LANES)
        wcol = tsm["wi"][:, :n_idx].reshape(bd, dec_seq * n_idx, 1)
        kinew = jnp.pad(tsm["kib"].reshape(bd, dec_seq, LANES), ((0, 0), (0, LANES - dec_seq), (0, 0)))
        bias_past, bias_new = _sample_select(page_table, cache_kidx[i], qi_rows, wcol, kinew, n_idx=n_idx,
                                             dec_seq=dec_seq, name="sample_select")
        q4 = jnp.pad(tsm["q"].reshape(bd, dec_seq, n_heads, LANES), ((0, 0), (0, 8 - dec_seq), (0, 0), (0, 0)))
        qblk = jnp.einsum("bthd,hg->bhdgt", q4, jnp.eye(n_heads, dtype=BF16)).reshape(bd, d_b, n_heads * 8)
        padrows = ((0, 0), (0, LANES - dec_seq), (0, 0))
        knew = jnp.pad(tsm["kb"].reshape(bd, dec_seq, d_b), padrows)
        vnew = jnp.pad(tsm["vb"].reshape(bd, dec_seq, d_b), padrows)
        attn_s8 = _sample_attention(page_table, cache_k2, cache_v2, i, qblk, bias_past, bias_new,
                                    knew, vnew, n_heads=n_heads, name="sample_attention")
        attn_s = attn_s8[:, :dec_seq].reshape(ms, d_b).astype(BF16)
        ys_new = _finish(ys, p_sample[i].reshape(ms, -1), tsm, attn_s, "sample", w)

        outs["kp"].append(tp["k"].reshape(batch, seq, n_heads, LANES))
        outs["vp"].append(tp["v"].reshape(batch, seq, n_heads, LANES))
        outs["kip"].append(tp["ki"].reshape(batch, seq, LANES))
        outs["ks"].append(tsm["k"].reshape(bd, dec_seq, n_heads, LANES))
        outs["vs"].append(tsm["v"].reshape(bd, dec_seq, n_heads, LANES))
        outs["kis"].append(tsm["ki"].reshape(bd, dec_seq, LANES))
        outs["vas"].append(tsm["a"][1].reshape(bd, dec_seq, d_a))
        yp, ys = yp_new, ys_new

    return (yp.reshape(batch, seq, d_model), ys.reshape(bd, dec_seq, d_model),
            jnp.stack(outs["kp"]), jnp.stack(outs["vp"]), jnp.stack(outs["kip"]),
            jnp.stack(outs["ks"]), jnp.stack(outs["vs"]), jnp.stack(outs["kis"]), jnp.stack(outs["vas"]))
```

```python
import functools

import jax
import jax.numpy as jnp
import numpy as np
from jax import lax
from jax.experimental import pallas as pl
from jax.experimental.pallas import tpu as pltpu

F32 = jnp.float32
BF16 = jnp.bfloat16
I32 = jnp.int32

LANES = 128
PAGE_SIZE = 128
CHUNK = 128
TOPK_MAX = 256
ROPE_THETA = 10000.0
EPS = 1e-6
NEG = -1e30
INT_MIN = -(2 ** 31)
VMEM_LIMIT = 56 * 1024 * 1024

_NT = (((1,), (1,)), ((), ()))


def _cparams(sem):
    return pltpu.CompilerParams(dimension_semantics=sem, vmem_limit_bytes=VMEM_LIMIT)


def _sigmoid(x):
    return 1.0 / (1.0 + jnp.exp(-x))


def _pick(n, cands):
    for c in cands:
        if n % c == 0:
            return c
    return n


def _rmsnorm_kernel(x_ref, g_ref, o_ref):
    x = x_ref[...]
    ms = jnp.mean(x * x, axis=-1, keepdims=True)
    o_ref[...] = (x * lax.rsqrt(ms + EPS) * g_ref[...]).astype(o_ref.dtype)


def _rmsnorm(x, g, name):
    m, d = x.shape
    tm = _pick(m, (256, 128))
    return pl.pallas_call(
        _rmsnorm_kernel,
        grid=(m // tm,),
        in_specs=[pl.BlockSpec((tm, d), lambda i: (i, 0)),
                  pl.BlockSpec((1, d), lambda i: (0, 0))],
        out_specs=pl.BlockSpec((tm, d), lambda i: (i, 0)),
        out_shape=jax.ShapeDtypeStruct((m, d), BF16),
        compiler_params=_cparams(("parallel",)),
        name=name,
    )(x, g.reshape(1, d))


def _rope(x, cos, sin_signed):
    return x * cos + pltpu.roll(x, LANES // 2, axis=1) * sin_signed


def _proj_kernel(*refs, kind, n_f32, n_bf16, tn, scale, t_chunk):
    h_ref, w_ref = refs[0], refs[1]
    acc = lax.dot_general(h_ref[...], w_ref[...], _NT, preferred_element_type=F32)
    if t_chunk:
        t_ref, refs = refs[-1], refs[:-1]
        for a in range(acc.shape[0] // t_chunk):
            t_ref[a] = acc[a * t_chunk:(a + 1) * t_chunk, :].T.astype(BF16)
    pos = 2
    if kind == "norm_rope":
        g_ref, cos_ref, sin_ref = refs[pos:pos + 3]
        pos += 3
    elif kind in ("rope", "kiwi"):
        cos_ref, sin_ref = refs[pos:pos + 2]
        pos += 2
    outs = refs[pos:]
    if kind == "kiwi":
        ki = _rope(acc[:, :LANES], cos_ref[...], sin_ref[...])
        outs[0][...] = ki
        outs[1][...] = ki.astype(BF16)
        outs[2][...] = acc[:, LANES:] * scale
        return
    for j in range(tn // LANES):
        sl = slice(j * LANES, (j + 1) * LANES)
        x = acc[:, sl]
        if kind == "norm_rope":
            ms = jnp.mean(x * x, axis=-1, keepdims=True)
            x = x * lax.rsqrt(ms + EPS) * g_ref[:, sl]
            x = _rope(x, cos_ref[...], sin_ref[...])
        elif kind == "rope":
            x = _rope(x, cos_ref[...], sin_ref[...])
        elif kind == "silu":
            x = x * _sigmoid(x)
        k = 0
        for _ in range(n_f32):
            outs[k][:, sl] = x
            k += 1
        for _ in range(n_bf16):
            outs[k][:, sl] = x.astype(BF16)
            k += 1


def _proj(h, w, *, kind, n_f32, n_bf16, name, col_off, n, gain=None, cos=None, sin=None, scale=1.0, t_chunk=0):
    m, kdim = h.shape
    tm = _pick(m, (1024, 512, 256, 128))
    assert t_chunk == 0 or tm % t_chunk == 0
    tn = 256 if kind == "kiwi" else _pick(np.gcd(n, col_off) if col_off else n, (512, 256, 128))
    assert col_off % tn == 0 and n % tn == 0
    blk_off = col_off // tn
    in_specs = [pl.BlockSpec((tm, kdim), lambda i, j: (i, 0)),
                pl.BlockSpec((tn, kdim), lambda i, j: (j + blk_off, 0))]
    args = [h, w]
    if kind == "norm_rope":
        in_specs.append(pl.BlockSpec((1, tn), lambda i, j: (0, j)))
        args.append(gain)
    if kind in ("norm_rope", "rope", "kiwi"):
        in_specs += [pl.BlockSpec((tm, LANES), lambda i, j: (i, 0))] * 2
        args += [cos, sin]
    if kind == "kiwi":
        out_shape = [jax.ShapeDtypeStruct((m, LANES), F32), jax.ShapeDtypeStruct((m, LANES), BF16),
                     jax.ShapeDtypeStruct((m, LANES), F32)]
        out_specs = [pl.BlockSpec((tm, LANES), lambda i, j: (i, 0))] * 3
    else:
        out_shape = ([jax.ShapeDtypeStruct((m, n), F32)] * n_f32 + [jax.ShapeDtypeStruct((m, n), BF16)] * n_bf16)
        out_specs = [pl.BlockSpec((tm, tn), lambda i, j: (i, j))] * (n_f32 + n_bf16)
    if t_chunk:
        out_shape = out_shape + [jax.ShapeDtypeStruct((m // t_chunk, n, t_chunk), BF16)]
        out_specs = out_specs + [pl.BlockSpec((tm // t_chunk, tn, t_chunk), lambda i, j: (i, j, 0))]
    return pl.pallas_call(
        functools.partial(_proj_kernel, kind=kind, n_f32=n_f32, n_bf16=n_bf16, tn=tn, scale=scale,
                          t_chunk=t_chunk),
        grid=(m // tm, n // tn),
        in_specs=in_specs,
        out_specs=out_specs,
        out_shape=out_shape,
        compiler_params=_cparams(("parallel", "arbitrary")),
        name=name,
    )(*args)


def _branch_a_kernel(h_ref, w_ref, gva_ref, bva_ref, ws_ref, bs_ref, *rest, tm, tn, d_a, emit_vn):
    if emit_vn:
        a_ref, vn_ref, scr = rest
    else:
        a_ref, scr = rest
    n = pl.program_id(1)
    nseg = d_a // tn
    per = tn // LANES
    scr[n] = lax.dot_general(h_ref[...], w_ref[...], _NT, preferred_element_type=F32)

    @pl.when(n == 3 * nseg - 1)
    def _():
        s1 = jnp.zeros((tm, 1), F32)
        for j in range(nseg):
            s1 = s1 + jnp.sum(scr[nseg + j], axis=-1, keepdims=True)
        mu = s1 / d_a
        s2 = jnp.zeros((tm, 1), F32)
        for j in range(nseg):
            dlt = scr[nseg + j] - mu
            s2 = s2 + jnp.sum(dlt * dlt, axis=-1, keepdims=True)
        rstd = lax.rsqrt(s2 / d_a + EPS)
        for g in range(d_a // LANES):
            t_idx, sl = g // per, slice((g % per) * LANES, (g % per + 1) * LANES)
            gsl = slice(g * LANES, (g + 1) * LANES)
            vn = (scr[nseg + t_idx][:, sl] - mu) * rstd * gva_ref[:, gsl] + bva_ref[:, gsl]
            if emit_vn:
                vn_ref[:, gsl] = vn
            vb = vn.astype(BF16)
            u = scr[t_idx][:, sl]
            z = scr[2 * nseg + t_idx][:, sl]
            for c in range(tm // CHUNK):
                rs = slice(c * CHUNK, (c + 1) * CHUNK)
                mixed = jnp.dot(ws_ref[g], vb[rs], preferred_element_type=F32) + bs_ref[g]
                a_ref[rs, gsl] = (u[rs] * mixed * (z[rs] * _sigmoid(z[rs]))).astype(BF16)


def _branch_a(h, w_a, g_va, b_va, ws, bs, *, emit_vn, name):
    m, kdim = h.shape
    d_a = g_va.shape[0]
    tm = _pick(m, (512, 256, 128))
    tn = _pick(d_a, (512, 256, 128))
    ng = d_a // LANES
    out_shape = [jax.ShapeDtypeStruct((m, d_a), BF16)]
    out_specs = [pl.BlockSpec((tm, d_a), lambda i, j: (i, 0))]
    if emit_vn:
        out_shape.append(jax.ShapeDtypeStruct((m, d_a), F32))
        out_specs.append(pl.BlockSpec((tm, d_a), lambda i, j: (i, 0)))
    return pl.pallas_call(
        functools.partial(_branch_a_kernel, tm=tm, tn=tn, d_a=d_a, emit_vn=emit_vn),
        grid=(m // tm, 3 * d_a // tn),
        in_specs=[pl.BlockSpec((tm, kdim), lambda i, j: (i, 0)),
                  pl.BlockSpec((tn, kdim), lambda i, j: (j, 0)),
                  pl.BlockSpec((1, d_a), lambda i, j: (0, 0)),
                  pl.BlockSpec((1, d_a), lambda i, j: (0, 0)),
                  pl.BlockSpec((ng, CHUNK, CHUNK), lambda i, j: (0, 0, 0)),
                  pl.BlockSpec((ng, CHUNK, LANES), lambda i, j: (0, 0, 0))],
        out_specs=out_specs,
        out_shape=out_shape,
        scratch_shapes=[pltpu.VMEM((3 * d_a // tn, tm, tn), F32)],
        compiler_params=_cparams(("parallel", "arbitrary")),
        name=name,
    )(h, w_a, g_va.reshape(1, d_a), b_va.reshape(1, d_a), ws, bs)


def _sortable_key(score):
    bits = lax.bitcast_convert_type(score, I32)
    return jnp.where(bits < 0, (bits ^ 0x7FFFFFFF) + 1, bits)


def _kth_largest_key(count_ge, shape, topk):
    n_valid = count_ge(jnp.full(shape, INT_MIN + 1, I32))

    def unresolved(cnt):
        return (jnp.max(jnp.where(jnp.logical_and(cnt != topk, n_valid >= topk), 1.0, 0.0)) > 0.0).astype(I32)

    def cond(state):
        b, _, _, go = state
        return jnp.logical_and(b < 32, go > 0)

    def body(state):
        b, lo, cnt_lo, _ = state
        cand = lo + lax.shift_left(jnp.int32(1), jnp.int32(31) - b)
        cnt = count_ge(cand)
        take = cnt >= topk
        lo = jnp.where(take, cand, lo)
        cnt_lo = jnp.where(take, cnt, cnt_lo)
        return b + 1, lo, cnt_lo, unresolved(cnt_lo)

    lo0 = jnp.full(shape, INT_MIN, I32)
    cnt0 = jnp.full(shape, 2.0 ** 31, F32)
    _, thr, n_ge, _ = lax.while_loop(cond, body, (jnp.int32(0), lo0, cnt0, jnp.int32(1)))
    return thr, n_ge


def _tie_index_cut(count_eq_below, need, shape, nbits):
    def body(b, ans):
        cand = ans + lax.shift_left(jnp.int32(1), jnp.int32(nbits - 1) - b)
        return jnp.where(count_eq_below(cand) < need, cand, ans)
    return lax.fori_loop(0, nbits, body, jnp.zeros(shape, I32))


def _idx_kernel(qi_ref, wi_ref, ki_ref, mask_ref, key_scr, wb_scr, *, tq, kc, n_idx, n_chunks_total, topk, nbits):
    i = pl.program_id(0)
    nch = i + 1
    per = kc // LANES
    for h in range(n_idx):
        wb_scr[h] = jnp.broadcast_to(wi_ref[:, h:h + 1], (tq, LANES))
    q_pos = i * tq + lax.broadcasted_iota(I32, (1, tq), 1)
    sub = lax.broadcasted_iota(I32, (kc, 1), 0)

    def score_chunk(c, carry):
        kic = ki_ref[pl.ds(pl.multiple_of(c * kc, kc), kc), :]
        acc = [jnp.zeros((tq, LANES), F32) for _ in range(per)]
        for h in range(n_idx):
            d = lax.dot_general(qi_ref[:, h * LANES:(h + 1) * LANES], kic, _NT, preferred_element_type=F32)
            w = wb_scr[h]
            for j in range(per):
                acc[j] = acc[j] + jnp.maximum(d[:, j * LANES:(j + 1) * LANES], 0.0) * w
        score_t = jnp.concatenate(acc, axis=1).T
        key_scr[c] = jnp.where(c * kc + sub <= q_pos, _sortable_key(score_t), INT_MIN)
        return carry

    lax.fori_loop(0, nch, score_chunk, 0)

    def count(pred):
        def body(c, cnt):
            hit = jnp.where(pred(key_scr[c], c * kc + sub), 1.0, 0.0)
            return cnt + jnp.sum(hit.reshape(kc // 8, 8, tq), axis=0)
        cnt = lax.fori_loop(0, nch, body, jnp.zeros((8, tq), F32))
        return jnp.sum(cnt, axis=0, keepdims=True)

    def write_mask(pred):
        def body(c, carry):
            mask_ref[0, c] = jnp.where(pred(key_scr[c], c * kc + sub), 1.0, 0.0).astype(BF16)
            return carry
        lax.fori_loop(0, nch, body, 0)

    thr, n_ge = _kth_largest_key(lambda cand: count(lambda k, kidx: k >= cand), (1, tq), topk)
    tie = jnp.logical_and(n_ge > topk, thr > INT_MIN)
    thr_valid = jnp.maximum(thr, INT_MIN + 1)
    has_tie = jnp.max(jnp.where(tie, 1.0, 0.0)) > 0.0

    def write_fast():
        write_mask(lambda k, kidx: k >= thr_valid)

    def write_ties():
        n_gt = count(lambda k, kidx: k > thr)
        need = topk - n_gt
        cut = _tie_index_cut(lambda cand: count(lambda k, kidx: jnp.logical_and(k == thr, kidx < cand)),
                             need, (1, tq), nbits)
        cut = jnp.where(tie, cut, jnp.int32(2 ** 30))
        write_mask(lambda k, kidx: jnp.logical_and(k >= thr_valid, jnp.logical_or(k > thr, kidx <= cut)))

    lax.cond(has_tie, write_ties, write_fast)

    def fill(c, carry):
        mask_ref[0, c] = jnp.zeros((kc, tq), BF16)
        return carry
    lax.fori_loop(nch, n_chunks_total, fill, 0)


def _prompt_select(qi, wi, ki, *, n_idx, name):
    s = qi.shape[0]
    tq = kc = _pick(s, (256, 128))
    nq = s // tq
    topk = min(TOPK_MAX, s // 4)
    nbits = max(1, int(np.ceil(np.log2(s))))
    return pl.pallas_call(
        functools.partial(_idx_kernel, tq=tq, kc=kc, n_idx=n_idx, n_chunks_total=nq, topk=topk, nbits=nbits),
        grid=(nq,),
        in_specs=[pl.BlockSpec((tq, n_idx * LANES), lambda i: (i, 0)),
                  pl.BlockSpec((tq, LANES), lambda i: (i, 0)),
                  pl.BlockSpec((s, LANES), lambda i: (0, 0))],
        out_specs=pl.BlockSpec((1, nq, kc, tq), lambda i: (i, 0, 0, 0)),
        out_shape=jax.ShapeDtypeStruct((nq, nq, kc, tq), BF16),
        scratch_shapes=[pltpu.VMEM((nq, kc, tq), I32), pltpu.VMEM((n_idx, tq, LANES), F32)],
        compiler_params=_cparams(("parallel",)),
        name=name,
    )(qi, wi, ki)


def _attn_kernel(shift_ref, q_ref, k_ref, vt_ref, mask_ref, o_ref, m_scr, acc_scr, *, rq, rk, kc, hps, scale2,
                 fixed_shift):
    i = pl.program_id(1)
    acc_scr[...] = jnp.zeros(acc_scr.shape, F32)
    if not fixed_shift:
        m_scr[...] = jnp.full(m_scr.shape, NEG, F32)
    ones = jnp.ones((acc_scr.shape[1] - LANES, kc), BF16)
    shift = shift_ref[0]

    def body(c, carry):
        keep = jnp.concatenate(
            [jnp.concatenate([mask_ref[a, c * rk + b] for a in range(rq)], axis=1) for b in range(rk)], axis=0)
        rows = pl.ds(pl.multiple_of(c * kc, kc), kc)
        heads = [slice(hh * LANES, (hh + 1) * LANES) for hh in range(hps)]
        logits = [lax.dot_general(k_ref[rows, sl], q_ref[:, sl], _NT, preferred_element_type=F32) for sl in heads]
        for hh, sl in enumerate(heads):
            v_ext = jnp.concatenate([vt_ref[c, sl, :], ones], axis=0)
            if fixed_shift:
                p = jnp.exp2(logits[hh] * scale2 - shift).astype(BF16) * keep
                acc_scr[hh] = acc_scr[hh] + jnp.dot(v_ext, p, preferred_element_type=F32)
            else:
                s = logits[hh] * scale2 + (keep.astype(F32) - 1.0) * (-NEG)
                m = m_scr[hh]
                m_new = jnp.maximum(m, jnp.max(s, axis=0, keepdims=True))
                alpha = jnp.exp2(m - m_new)
                p = jnp.exp2(s - m_new).astype(BF16)
                acc_scr[hh] = alpha * acc_scr[hh] + jnp.dot(v_ext, p, preferred_element_type=F32)
                m_scr[hh] = m_new
        return carry

    lax.fori_loop(0, i + 1, body, 0)
    for hh in range(hps):
        acc = acc_scr[hh]
        o_ref[:, hh * LANES:(hh + 1) * LANES] = (acc[:LANES] / acc[LANES:LANES + 1]).T.astype(o_ref.dtype)


_FIXED_SHIFT_MAX = 50.0


def _prompt_attention(q, k, vt, mask, g_q, g_k, *, name):
    s, d_b = q.shape
    nqs, nks, kcs, tqs = mask.shape
    nkc, _, kc = vt.shape
    tq = kc
    rq, rk = tq // tqs, kc // kcs
    hps = _pick(d_b // LANES, (4, 2, 1))
    wblk = hps * LANES
    once = pl.Buffered(1)
    scale2 = float(LANES ** -0.5 * np.log2(np.e))
    bound = (1.01 * LANES * scale2) * jnp.max(jnp.abs(g_q)) * jnp.max(jnp.abs(g_k))

    def run(fixed_shift):
        return pl.pallas_call(
            functools.partial(_attn_kernel, rq=rq, rk=rk, kc=kc, hps=hps, scale2=scale2, fixed_shift=fixed_shift),
            grid=(d_b // wblk, s // tq),
            in_specs=[pl.BlockSpec(memory_space=pltpu.SMEM),
                      pl.BlockSpec((tq, wblk), lambda h, i: (i, h)),
                      pl.BlockSpec((s, wblk), lambda h, i: (0, h), pipeline_mode=once),
                      pl.BlockSpec((nkc, wblk, kc), lambda h, i: (0, h, 0), pipeline_mode=once),
                      pl.BlockSpec((rq, nks, kcs, tqs), lambda h, i: (i, 0, 0, 0))],
            out_specs=pl.BlockSpec((tq, wblk), lambda h, i: (i, h)),
            out_shape=jax.ShapeDtypeStruct((s, d_b), BF16),
            scratch_shapes=[pltpu.VMEM((hps, 1, tq), F32), pltpu.VMEM((hps, LANES + 16, tq), F32)],
            compiler_params=_cparams(("parallel", "arbitrary")),
            name=name + ("_fixed_shift" if fixed_shift else "_online"),
        )(bound.reshape(1).astype(F32), q, k, vt, mask)

    return lax.cond(bound <= _FIXED_SHIFT_MAX, lambda: run(True), lambda: run(False))


def _sample_idx_kernel(pt_ref, *refs, g_pages, n_steps, n_idx, dec_seq, topk, nbits, past, att_keys):
    del pt_ref
    pages = refs[:g_pages]
    qi_ref, wcol_ref, kinew_ref, bias_past_ref, bias_new_ref, kb_scr, key_scr, keyn_scr = refs[g_pages:]
    j = pl.program_id(1)
    pg = g_pages * PAGE_SIZE
    rows = 8
    row_id = lax.broadcasted_iota(I32, (rows, 1), 0)

    def scores_of(kmat):
        d = lax.dot_general(qi_ref[0], kmat, _NT, preferred_element_type=F32)
        wr = jnp.maximum(d, 0.0) * wcol_ref[0]
        s4 = jnp.sum(wr.reshape(dec_seq, n_idx, kmat.shape[0]), axis=1)
        return jnp.concatenate([s4, jnp.zeros((rows - dec_seq, kmat.shape[0]), F32)], axis=0)

    for r in range(g_pages):
        kb_scr[r * PAGE_SIZE:(r + 1) * PAGE_SIZE, :] = pages[r][0].astype(BF16)
    key_scr[j] = jnp.where(row_id < dec_seq, _sortable_key(scores_of(kb_scr[...])), INT_MIN)

    @pl.when(j == n_steps - 1)
    def _():
        sn = scores_of(kinew_ref[0])
        coln = lax.broadcasted_iota(I32, (1, LANES), 1)
        okn = jnp.logical_and(jnp.logical_and(coln < dec_seq, coln <= row_id), row_id < dec_seq)
        keyn_scr[...] = jnp.where(okn, _sortable_key(sn), INT_MIN)

        def count(pred):
            cnt = jnp.zeros((rows, LANES), F32)
            for c in range(n_steps):
                p = pred(key_scr[c], c * pg + lax.broadcasted_iota(I32, (1, pg), 1))
                for jj in range(pg // LANES):
                    cnt = cnt + jnp.where(p[:, jj * LANES:(jj + 1) * LANES], 1.0, 0.0)
            cnt = cnt + jnp.where(pred(keyn_scr[...], past + coln), 1.0, 0.0)
            return jnp.sum(cnt, axis=-1, keepdims=True)

        thr, n_ge = _kth_largest_key(lambda cand: count(lambda k, col: k >= cand), (rows, 1), topk)
        n_gt = count(lambda k, col: k > thr)
        tie = jnp.logical_and(n_ge > topk, thr > INT_MIN)
        thr_valid = jnp.maximum(thr, INT_MIN + 1)
        need = topk - n_gt
        cut = _tie_index_cut(
            lambda cand: count(lambda k, col: jnp.logical_and(k == thr, col < cand)), need, (rows, 1), nbits)
        cut = jnp.where(tie, cut, jnp.int32(2 ** 30))

        def sel(k, col):
            return jnp.logical_and(k >= thr_valid, jnp.logical_or(k > thr, col <= cut))

        n_sub = pg // att_keys
        for c in range(n_steps):
            b = jnp.where(sel(key_scr[c], c * pg + lax.broadcasted_iota(I32, (1, pg), 1)), 0.0, NEG)
            for u in range(n_sub):
                bias_past_ref[0, n_sub * c + u] = b[:, u * att_keys:(u + 1) * att_keys]
        bias_new_ref[0] = jnp.where(sel(keyn_scr[...], past + coln), 0.0, NEG)


def _sample_select(page_table, cache_kidx, qi_rows, wcol, kinew, *, n_idx, dec_seq, name):
    bd, n_pages = page_table.shape
    g_pages = _pick(n_pages, (16, 8, 4, 2, 1))
    n_steps = n_pages // g_pages
    pg = g_pages * PAGE_SIZE
    att_keys = _pick(pg, (4 * PAGE_SIZE, 2 * PAGE_SIZE, PAGE_SIZE))
    n_att = n_steps * (pg // att_keys)
    past = n_pages * PAGE_SIZE
    topk = min(TOPK_MAX, (past + dec_seq) // 4)
    nbits = max(1, int(np.ceil(np.log2(past + LANES))))
    rq = dec_seq * n_idx

    def page_spec(r):
        return pl.BlockSpec((1, PAGE_SIZE, LANES), lambda b, j, pt: (pt[b, j * g_pages + r], 0, 0))

    grid_spec = pltpu.PrefetchScalarGridSpec(
        num_scalar_prefetch=1,
        grid=(bd, n_steps),
        in_specs=[page_spec(r) for r in range(g_pages)] + [
            pl.BlockSpec((1, rq, LANES), lambda b, j, pt: (b, 0, 0)),
            pl.BlockSpec((1, rq, 1), lambda b, j, pt: (b, 0, 0)),
            pl.BlockSpec((1, LANES, LANES), lambda b, j, pt: (b, 0, 0))],
        out_specs=[pl.BlockSpec((1, n_att, 8, att_keys), lambda b, j, pt: (b, 0, 0, 0)),
                   pl.BlockSpec((1, 8, LANES), lambda b, j, pt: (b, 0, 0))],
        scratch_shapes=[pltpu.VMEM((pg, LANES), BF16), pltpu.VMEM((n_steps, 8, pg), I32),
                        pltpu.VMEM((8, LANES), I32)],
    )
    return pl.pallas_call(
        functools.partial(_sample_idx_kernel, g_pages=g_pages, n_steps=n_steps, n_idx=n_idx,
                          dec_seq=dec_seq, topk=topk, nbits=nbits, past=past, att_keys=att_keys),
        grid_spec=grid_spec,
        out_shape=[jax.ShapeDtypeStruct((bd, n_att, 8, att_keys), F32),
                   jax.ShapeDtypeStruct((bd, 8, LANES), F32)],
        compiler_params=_cparams(("parallel", "arbitrary")),
        name=name,
    )(page_table, *([cache_kidx] * g_pages), qi_rows, wcol, kinew)


def _sample_attn_kernel(pt_ref, *refs, g_pages, n_steps, n_heads, scale):
    del pt_ref
    kpages = refs[:g_pages]
    vpages = refs[g_pages:2 * g_pages]
    (qblk_ref, bias_ref, biasn_ref, knew_ref, vnew_ref, o_ref,
     kb_scr, vb_scr, m_scr, l_scr, acc_scr) = refs[2 * g_pages:]
    j = pl.program_id(1)
    rows = 8 * n_heads

    @pl.when(j == 0)
    def _():
        m_scr[...] = jnp.full(m_scr.shape, NEG, F32)
        l_scr[...] = jnp.zeros(l_scr.shape, F32)
        acc_scr[...] = jnp.zeros(acc_scr.shape, F32)

    def logits_t(kb):
        return jnp.dot(kb, qblk_ref[0], preferred_element_type=F32)

    def update(st, vb, bias8):
        s = st.T * scale + jnp.concatenate([bias8] * n_heads, axis=0)
        m = m_scr[...]
        m_new = jnp.maximum(m, jnp.max(s, axis=-1, keepdims=True))
        alpha = jnp.exp2(m - m_new)
        p = jnp.exp2(s - m_new)
        l_scr[...] = alpha * l_scr[...] + jnp.sum(p, axis=-1, keepdims=True)
        acc_scr[...] = alpha * acc_scr[...] + jnp.dot(p.astype(BF16), vb, preferred_element_type=F32)
        m_scr[...] = m_new

    def gather_heads(pages, dst):
        kg = 16
        for r in range(g_pages):
            for g in range(PAGE_SIZE // kg):
                blk = pages[r][0, 0, g * kg * n_heads:(g + 1) * kg * n_heads, :].astype(BF16)
                by_head = pltpu.einshape("khd->hkd", blk.reshape(kg, n_heads, LANES))
                ks = slice(r * PAGE_SIZE + g * kg, r * PAGE_SIZE + (g + 1) * kg)
                for h in range(n_heads):
                    dst[ks, h * LANES:(h + 1) * LANES] = by_head[h]

    gather_heads(kpages, kb_scr)
    st = logits_t(kb_scr[...])
    gather_heads(vpages, vb_scr)
    update(st, vb_scr[...], bias_ref[0, 0])

    @pl.when(j == n_steps - 1)
    def _():
        update(logits_t(knew_ref[0]), vnew_ref[0], biasn_ref[0])
        for h in range(n_heads):
            rs = slice(h * 8, (h + 1) * 8)
            cs = slice(h * LANES, (h + 1) * LANES)
            o_ref[0, :, cs] = acc_scr[rs, cs] / l_scr[rs, :]


def _sample_attention(page_table, cache_k, cache_v, layer, qblk, bias_past, bias_new, knew, vnew, *, n_heads, name):
    bd, n_pages = page_table.shape
    d_b = n_heads * LANES
    n_steps = bias_past.shape[1]
    g_pages = n_pages // n_steps
    pg = g_pages * PAGE_SIZE
    rows = 8 * n_heads

    def page_spec(r):
        return pl.BlockSpec((1, 1, PAGE_SIZE * n_heads, LANES),
                            lambda b, j, pt: (layer, pt[b, j * g_pages + r], 0, 0))

    grid_spec = pltpu.PrefetchScalarGridSpec(
        num_scalar_prefetch=1,
        grid=(bd, n_steps),
        in_specs=[page_spec(r) for r in range(g_pages)] * 2 + [
            pl.BlockSpec((1, d_b, rows), lambda b, j, pt: (b, 0, 0)),
            pl.BlockSpec((1, 1, 8, pg), lambda b, j, pt: (b, j, 0, 0)),
            pl.BlockSpec((1, 8, LANES), lambda b, j, pt: (b, 0, 0)),
            pl.BlockSpec((1, LANES, d_b), lambda b, j, pt: (b, 0, 0)),
            pl.BlockSpec((1, LANES, d_b), lambda b, j, pt: (b, 0, 0))],
        out_specs=pl.BlockSpec((1, 8, d_b), lambda b, j, pt: (b, 0, 0)),
        scratch_shapes=[pltpu.VMEM((pg, d_b), BF16), pltpu.VMEM((pg, d_b), BF16),
                        pltpu.VMEM((rows, 1), F32), pltpu.VMEM((rows, 1), F32), pltpu.VMEM((rows, d_b), F32)],
    )
    return pl.pallas_call(
        functools.partial(_sample_attn_kernel, g_pages=g_pages, n_steps=n_steps, n_heads=n_heads,
                          scale=float(LANES ** -0.5 * np.log2(np.e))),
        grid_spec=grid_spec,
        out_shape=jax.ShapeDtypeStruct((bd, 8, d_b), F32),
        compiler_params=_cparams(("parallel", "arbitrary")),
        name=name,
    )(page_table, *([cache_k] * g_pages), *([cache_v] * g_pages), qblk, bias_past, bias_new, knew, vnew)


def _merge_kernel(h_ref, a_ref, attn_ref, zs_ref, wga_ref, wgb_ref, wua_ref, wub_ref, o_ref, bg_scr):
    @pl.when(pl.program_id(1) == 0)
    def _():
        bg_scr[...] = (attn_ref[...].astype(F32) * zs_ref[...].astype(F32)).astype(BF16)

    h = h_ref[...]
    ga = lax.dot_general(h, wga_ref[...], _NT, preferred_element_type=F32)
    gb = lax.dot_general(h, wgb_ref[...], _NT, preferred_element_type=F32)
    a_up = jnp.dot(a_ref[...], wua_ref[...].astype(BF16), preferred_element_type=F32)
    b_up = jnp.dot(bg_scr[...], wub_ref[...].astype(BF16), preferred_element_type=F32)
    o_ref[...] = (_sigmoid(ga) * a_up + _sigmoid(gb) * b_up).astype(o_ref.dtype)


def _merge(h, a_gated, attn, zs, w_ga, w_gb, w_ua, w_ub, *, name):
    m, d = h.shape
    d_a, d_b = a_gated.shape[1], attn.shape[1]
    tm = _pick(m, (512, 256, 128))
    tn = _pick(d, (256, 128))
    row = lambda i, j: (i, 0)
    col = lambda i, j: (0, j)
    return pl.pallas_call(
        _merge_kernel,
        grid=(m // tm, d // tn),
        in_specs=[pl.BlockSpec((tm, d), row), pl.BlockSpec((tm, d_a), row),
                  pl.BlockSpec((tm, d_b), row), pl.BlockSpec((tm, d_b), row),
                  pl.BlockSpec((tn, d), lambda i, j: (j, 0)), pl.BlockSpec((tn, d), lambda i, j: (j, 0)),
                  pl.BlockSpec((d_a, tn), col), pl.BlockSpec((d_b, tn), col)],
        out_specs=pl.BlockSpec((tm, tn), lambda i, j: (i, j)),
        out_shape=jax.ShapeDtypeStruct((m, d), BF16),
        scratch_shapes=[pltpu.VMEM((tm, d_b), BF16)],
        compiler_params=_cparams(("parallel", "arbitrary")),
        name=name,
    )(h, a_gated, attn, zs, w_ga, w_gb, w_ua, w_ub)


def _resid_kernel(mg_ref, w_ref, x_ref, o_ref):
    o_ref[...] = x_ref[...] + jnp.dot(mg_ref[...], w_ref[...].astype(BF16), preferred_element_type=F32)


def _resid_proj(merged, w_o, x, *, name):
    m, d = x.shape
    tm = _pick(m, (1024, 512, 256, 128))
    tn = _pick(d, (512, 256, 128))
    return pl.pallas_call(
        _resid_kernel,
        grid=(m // tm, d // tn),
        in_specs=[pl.BlockSpec((tm, merged.shape[1]), lambda i, j: (i, 0)),
                  pl.BlockSpec((merged.shape[1], tn), lambda i, j: (0, j)),
                  pl.BlockSpec((tm, tn), lambda i, j: (i, j))],
        out_specs=pl.BlockSpec((tm, tn), lambda i, j: (i, j)),
        out_shape=jax.ShapeDtypeStruct((m, d), F32),
        compiler_params=_cparams(("parallel", "arbitrary")),
        name=name,
    )(merged, w_o, x)


def _ple_kernel(hn_ref, wg_ref, p_ref, wp_ref, x1_ref, o_ref):
    gate = _sigmoid(jnp.dot(hn_ref[...], wg_ref[...].astype(BF16), preferred_element_type=F32))
    emb = jnp.dot(p_ref[...], wp_ref[...].astype(BF16), preferred_element_type=F32)
    o_ref[...] = x1_ref[...] + gate * emb


def _ple(hn, w_gate, p, w_proj, x1, *, name):
    m, d = x1.shape
    tm = _pick(m, (1024, 512, 256, 128))
    tn = _pick(d, (512, 256, 128))
    return pl.pallas_call(
        _ple_kernel,
        grid=(m // tm, d // tn),
        in_specs=[pl.BlockSpec((tm, d), lambda i, j: (i, 0)),
                  pl.BlockSpec((d, tn), lambda i, j: (0, j)),
                  pl.BlockSpec((tm, p.shape[1]), lambda i, j: (i, 0)),
                  pl.BlockSpec((p.shape[1], tn), lambda i, j: (0, j)),
                  pl.BlockSpec((tm, tn), lambda i, j: (i, j))],
        out_specs=pl.BlockSpec((tm, tn), lambda i, j: (i, j)),
        out_shape=jax.ShapeDtypeStruct((m, d), F32),
        compiler_params=_cparams(("parallel", "arbitrary")),
        name=name,
    )(hn, w_gate, p, w_proj, x1)


def _rope_tables(pos):
    half = LANES // 2
    inv = ROPE_THETA ** (-jnp.arange(half, dtype=F32) / half)
    ang = pos.astype(F32)[:, None] * inv[None, :]
    cos, sin = jnp.cos(ang), jnp.sin(ang)
    return jnp.concatenate([cos, cos], axis=-1), jnp.concatenate([-sin, sin], axis=-1)


def _input_weights(w_in, d_model, d_a, d_b, n_idx):
    widths = [3 * d_a, d_b, d_b, d_b, d_b, n_idx * LANES, LANES + n_idx, d_model, d_model]
    offs = [int(v) for v in np.concatenate([[0], np.cumsum(widths)])]
    w_t = jnp.swapaxes(w_in, 0, 1).astype(BF16)
    cols = dict(zip(("a", "q", "k", "v", "zb", "qi", "kiwi"), offs[:7]))
    return w_t, cols, w_t[offs[7]:offs[8]], w_t[offs[8]:offs[9]]


def _layer(x, p, pos, tag, w, ws, bs, *, n_heads, n_idx, emit_vn, v_chunk=0):
    m, d_model = x.shape
    cosf, sinf = _rope_tables(pos)
    h = _rmsnorm(x, w["g_norm"], f"{tag}_norm")
    w_all, cols, d_b = w["w_all"], w["cols"], n_heads * LANES
    a_out = _branch_a(h, w_all, w["g_va"], w["b_va"], ws, bs, emit_vn=emit_vn, name=f"{tag}_branch_a")
    gq = jnp.tile(w["g_q"], n_heads).reshape(1, -1)
    gk = jnp.tile(w["g_k"], n_heads).reshape(1, -1)
    (q_bf,) = _proj(h, w_all, col_off=cols["q"], n=d_b, kind="norm_rope", n_f32=0, n_bf16=1, gain=gq, cos=cosf,
                    sin=sinf, name=f"{tag}_q")
    k_f32, k_bf = _proj(h, w_all, col_off=cols["k"], n=d_b, kind="norm_rope", n_f32=1, n_bf16=1, gain=gk, cos=cosf,
                        sin=sinf, name=f"{tag}_k")
    if v_chunk:
        v_f32, v_bf = _proj(h, w_all, col_off=cols["v"], n=d_b, kind="plain", n_f32=1, n_bf16=0, t_chunk=v_chunk,
                            name=f"{tag}_v")
    else:
        v_f32, v_bf = _proj(h, w_all, col_off=cols["v"], n=d_b, kind="plain", n_f32=1, n_bf16=1, name=f"{tag}_v")
    (zs,) = _proj(h, w_all, col_off=cols["zb"], n=d_b, kind="silu", n_f32=0, n_bf16=1, name=f"{tag}_zb")
    (qi,) = _proj(h, w_all, col_off=cols["qi"], n=n_idx * LANES, kind="rope", n_f32=0, n_bf16=1, cos=cosf, sin=sinf,
                  name=f"{tag}_qi")
    ki_f32, ki_bf, wi = _proj(h, w_all, col_off=cols["kiwi"], n=2 * LANES, kind="kiwi", n_f32=0, n_bf16=0, cos=cosf,
                              sin=sinf, scale=float(n_idx ** -0.5 * LANES ** -0.5), name=f"{tag}_kiwi")
    return dict(h=h, a=a_out, q=q_bf, k=k_f32, kb=k_bf, v=v_f32, vb=v_bf, zs=zs, qi=qi, ki=ki_f32, kib=ki_bf, wi=wi)


def _finish(x, p, t, attn, tag, w):
    merged = _merge(t["h"], t["a"][0], attn, t["zs"], w["w_ga"], w["w_gb"], w["w_ua"], w["w_ub"], name=f"{tag}_merge")
    x1 = _resid_proj(merged, w["w_o"], x, name=f"{tag}_resid")
    hn = _rmsnorm(x1, w["g_ple"], f"{tag}_ple_norm")
    return _ple(hn, w["w_pg"], p.astype(BF16), w["w_pp"], x1, name=f"{tag}_ple")


def kernel(x_prompt, x_sample, p_prompt, p_sample, cache_k, cache_v, cache_kidx, page_table, g_norm, w_in, g_va,
           b_va, w_s, b_s, g_q, g_k, w_up_a, w_up_b, w_o, g_ple, w_ple_gate, w_ple_proj):
    depth = w_in.shape[0]
    batch, seq, d_model = x_prompt.shape
    bd, dec_seq, _ = x_sample.shape
    n_phys, _, n_heads, head_dim = cache_k.shape[1:]
    n_groups = w_s.shape[1]
    d_a, d_b = w_up_a.shape[1], w_up_b.shape[1]
    n_idx = w_in.shape[2] - (3 * d_a + 4 * d_b + LANES + 2 * d_model)
    n_idx = n_idx // (LANES + 1)
    n_pages = page_table.shape[1]
    past = n_pages * PAGE_SIZE
    assert batch == 1 and head_dim == LANES and cache_kidx.shape[-1] == LANES and d_a == n_groups * LANES
    assert seq % CHUNK == 0 and bd * dec_seq == CHUNK and dec_seq <= 8 and d_b == n_heads * LANES
    ms = bd * dec_seq

    yp = x_prompt.reshape(seq, d_model)
    ys = x_sample.reshape(ms, d_model)
    pos_p = jnp.arange(seq)
    pos_s = jnp.tile(past + jnp.arange(dec_seq), bd)
    outs = {k: [] for k in ("kp", "vp", "kip", "ks", "vs", "kis", "vas")}
    tril = jnp.tril(jnp.ones((CHUNK, CHUNK), bool))
    cache_k2 = cache_k.reshape(depth, n_phys, PAGE_SIZE * n_heads, LANES)
    cache_v2 = cache_v.reshape(depth, n_phys, PAGE_SIZE * n_heads, LANES)
    for i in range(depth):
        w_all, cols, w_ga, w_gb = _input_weights(w_in[i], d_model, d_a, d_b, n_idx)
        w = dict(w_all=w_all, cols=cols, w_ga=w_ga, w_gb=w_gb)
        w.update(g_norm=g_norm[i], g_va=g_va[i], b_va=b_va[i], g_q=g_q[i], g_k=g_k[i], g_ple=g_ple[i],
                 w_ua=w_up_a[i], w_ub=w_up_b[i], w_o=w_o[i], w_pg=w_ple_gate[i], w_pp=w_ple_proj[i])
        ws_p = jnp.where(tril[None], w_s[i], 0.0).astype(BF16)
        bs_p = jnp.broadcast_to(b_s[i][:, :, None], (n_groups, CHUNK, LANES))
        small = jnp.where(tril[None, :dec_seq, :dec_seq], w_s[i][:, :dec_seq, :dec_seq], 0.0)
        ws_s = jnp.einsum("ab,gts->gatbs", jnp.eye(bd, dtype=F32), small).reshape(n_groups, ms, ms).astype(BF16)
        bs_s = jnp.broadcast_to(jnp.tile(b_s[i][:, :dec_seq], (1, bd))[:, :, None], (n_groups, ms, LANES))

        tp = _layer(yp, p_prompt[i, 0], pos_p, "prompt", w, ws_p, bs_p, n_heads=n_heads, n_idx=n_idx, emit_vn=False,
                    v_chunk=_pick(seq, (512, 256, 128)))
        mask = _prompt_select(tp["qi"], tp["wi"], tp["kib"], n_idx=n_idx, name="prompt_select")
        attn_p = _prompt_attention(tp["q"], tp["kb"], tp["vb"], mask, g_q[i], g_k[i], name="prompt_attention")
        yp_new = _finish(yp, p_prompt[i, 0], tp, attn_p, "prompt", w)

        tsm = _layer(ys, p_sample[i].reshape(ms, -1), pos_s, "sample", w, ws_s, bs_s, n_heads=n_heads, n_idx=n_idx,
                     emit_vn=True)
        qi_rows = tsm["qi"].reshape(bd, dec_seq * n_idx, LANES)
        wcol = tsm["wi"][:, :n_idx].reshape(bd, dec_seq * n_idx, 1)
        kinew = jnp.pad(tsm["kib"].reshape(bd, dec_seq, LANES), ((0, 0), (0, LANES - dec_seq), (0, 0)))
        bias_past, bias_new = _sample_select(page_table, cache_kidx[i], qi_rows, wcol, kinew, n_idx=n_idx,
                                             dec_seq=dec_seq, name="sample_select")
        q4 = jnp.pad(tsm["q"].reshape(bd, dec_seq, n_heads, LANES), ((0, 0), (0, 8 - dec_seq), (0, 0), (0, 0)))
        qblk = jnp.einsum("bthd,hg->bhdgt", q4, jnp.eye(n_heads, dtype=BF16)).reshape(bd, d_b, n_heads * 8)
        padrows = ((0, 0), (0, LANES - dec_seq), (0, 0))
        knew = jnp.pad(tsm["kb"].reshape(bd, dec_seq, d_b), padrows)
        vnew = jnp.pad(tsm["vb"].reshape(bd, dec_seq, d_b), padrows)
        attn_s8 = _sample_attention(page_table, cache_k2, cache_v2, i, qblk, bias_past, bias_new,
                                    knew, vnew, n_heads=n_heads, name="sample_attention")
        attn_s = attn_s8[:, :dec_seq].reshape(ms, d_b).astype(BF16)
        ys_new = _finish(ys, p_sample[i].reshape(ms, -1), tsm, attn_s, "sample", w)

        outs["kp"].append(tp["k"].reshape(batch, seq, n_heads, LANES))
        outs["vp"].append(tp["v"].reshape(batch, seq, n_heads, LANES))
        outs["kip"].append(tp["ki"].reshape(batch, seq, LANES))
        outs["ks"].append(tsm["k"].reshape(bd, dec_seq, n_heads, LANES))
        outs["vs"].append(tsm["v"].reshape(bd, dec_seq, n_heads, LANES))
        outs["kis"].append(tsm["ki"].reshape(bd, dec_seq, LANES))
        outs["vas"].append(tsm["a"][1].reshape(bd, dec_seq, d_a))
        yp, ys = yp_new, ys_new

    return (yp.reshape(batch, seq, d_model), ys.reshape(bd, dec_seq, d_model),
            jnp.stack(outs["kp"]), jnp.stack(outs["vp"]), jnp.stack(outs["kip"]),
            jnp.stack(outs["ks"]), jnp.stack(outs["vs"]), jnp.stack(outs["kis"]), jnp.stack(outs["vas"]))
```

```python
import functools

import jax
import jax.numpy as jnp
import numpy as np
from jax import lax
from jax.experimental import pallas as pl
from jax.experimental.pallas import tpu as pltpu

F32 = jnp.float32
BF16 = jnp.bfloat16
I32 = jnp.int32

LANES = 128
PAGE_SIZE = 128
CHUNK = 128
TOPK_MAX = 256
ROPE_THETA = 10000.0
EPS = 1e-6
NEG = -1e30
INT_MIN = -(2 ** 31)
VMEM_LIMIT = 56 * 1024 * 1024

_NT = (((1,), (1,)), ((), ()))


def _cparams(sem):
    return pltpu.CompilerParams(dimension_semantics=sem, vmem_limit_bytes=VMEM_LIMIT)


def _sigmoid(x):
    return 1.0 / (1.0 + jnp.exp(-x))


def _pick(n, cands):
    for c in cands:
        if n % c == 0:
            return c
    return n


def _rmsnorm_kernel(x_ref, g_ref, o_ref):
    x = x_ref[...]
    ms = jnp.mean(x * x, axis=-1, keepdims=True)
    o_ref[...] = (x * lax.rsqrt(ms + EPS) * g_ref[...]).astype(o_ref.dtype)


def _rmsnorm(x, g, name):
    m, d = x.shape
    tm = _pick(m, (256, 128))
    return pl.pallas_call(
        _rmsnorm_kernel,
        grid=(m // tm,),
        in_specs=[pl.BlockSpec((tm, d), lambda i: (i, 0)),
                  pl.BlockSpec((1, d), lambda i: (0, 0))],
        out_specs=pl.BlockSpec((tm, d), lambda i: (i, 0)),
        out_shape=jax.ShapeDtypeStruct((m, d), BF16),
        compiler_params=_cparams(("parallel",)),
        name=name,
    )(x, g.reshape(1, d))


def _rope(x, cos, sin_signed):
    return x * cos + pltpu.roll(x, LANES // 2, axis=1) * sin_signed


def _proj_kernel(*refs, kind, n_f32, n_bf16, tn, scale, t_chunk):
    h_ref, w_ref = refs[0], refs[1]
    acc = lax.dot_general(h_ref[...], w_ref[...], _NT, preferred_element_type=F32)
    if t_chunk:
        t_ref, refs = refs[-1], refs[:-1]
        for a in range(acc.shape[0] // t_chunk):
            t_ref[a] = acc[a * t_chunk:(a + 1) * t_chunk, :].T.astype(BF16)
    pos = 2
    if kind == "norm_rope":
        g_ref, cos_ref, sin_ref = refs[pos:pos + 3]
        pos += 3
    elif kind in ("rope", "kiwi"):
        cos_ref, sin_ref = refs[pos:pos + 2]
        pos += 2
    outs = refs[pos:]
    if kind == "kiwi":
        ki = _rope(acc[:, :LANES], cos_ref[...], sin_ref[...])
        outs[0][...] = ki
        outs[1][...] = ki.astype(BF16)
        outs[2][...] = acc[:, LANES:] * scale
        return
    for j in range(tn // LANES):
        sl = slice(j * LANES, (j + 1) * LANES)
        x = acc[:, sl]
        if kind == "norm_rope":
            ms = jnp.mean(x * x, axis=-1, keepdims=True)
            x = x * lax.rsqrt(ms + EPS) * g_ref[:, sl]
            x = _rope(x, cos_ref[...], sin_ref[...])
        elif kind == "rope":
            x = _rope(x, cos_ref[...], sin_ref[...])
        elif kind == "silu":
            x = x * _sigmoid(x)
        k = 0
        for _ in range(n_f32):
            outs[k][:, sl] = x
            k += 1
        for _ in range(n_bf16):
            outs[k][:, sl] = x.astype(BF16)
            k += 1


def _proj(h, w, *, kind, n_f32, n_bf16, name, col_off, n, gain=None, cos=None, sin=None, scale=1.0, t_chunk=0):
    m, kdim = h.shape
    tm = _pick(m, (1024, 512, 256, 128))
    assert t_chunk == 0 or tm % t_chunk == 0
    tn = 256 if kind == "kiwi" else _pick(np.gcd(n, col_off) if col_off else n, (512, 256, 128))
    assert col_off % tn == 0 and n % tn == 0
    blk_off = col_off // tn
    in_specs = [pl.BlockSpec((tm, kdim), lambda i, j: (i, 0)),
                pl.BlockSpec((tn, kdim), lambda i, j: (j + blk_off, 0))]
    args = [h, w]
    if kind == "norm_rope":
        in_specs.append(pl.BlockSpec((1, tn), lambda i, j: (0, j)))
        args.append(gain)
    if kind in ("norm_rope", "rope", "kiwi"):
        in_specs += [pl.BlockSpec((tm, LANES), lambda i, j: (i, 0))] * 2
        args += [cos, sin]
    if kind == "kiwi":
        out_shape = [jax.ShapeDtypeStruct((m, LANES), F32), jax.ShapeDtypeStruct((m, LANES), BF16),
                     jax.ShapeDtypeStruct((m, LANES), F32)]
        out_specs = [pl.BlockSpec((tm, LANES), lambda i, j: (i, 0))] * 3
    else:
        out_shape = ([jax.ShapeDtypeStruct((m, n), F32)] * n_f32 + [jax.ShapeDtypeStruct((m, n), BF16)] * n_bf16)
        out_specs = [pl.BlockSpec((tm, tn), lambda i, j: (i, j))] * (n_f32 + n_bf16)
    if t_chunk:
        out_shape = out_shape + [jax.ShapeDtypeStruct((m // t_chunk, n, t_chunk), BF16)]
        out_specs = out_specs + [pl.BlockSpec((tm // t_chunk, tn, t_chunk), lambda i, j: (i, j, 0))]
    return pl.pallas_call(
        functools.partial(_proj_kernel, kind=kind, n_f32=n_f32, n_bf16=n_bf16, tn=tn, scale=scale,
                          t_chunk=t_chunk),
        grid=(m // tm, n // tn),
        in_specs=in_specs,
        out_specs=out_specs,
        out_shape=out_shape,
        compiler_params=_cparams(("parallel", "arbitrary")),
        name=name,
    )(*args)


def _branch_a_kernel(h_ref, w_ref, gva_ref, bva_ref, ws_ref, bs_ref, *rest, tm, tn, d_a, emit_vn):
    if emit_vn:
        a_ref, vn_ref, scr = rest
    else:
        a_ref, scr = rest
    n = pl.program_id(1)
    nseg = d_a // tn
    per = tn // LANES
    scr[n] = lax.dot_general(h_ref[...], w_ref[...], _NT, preferred_element_type=F32)

    @pl.when(n == 3 * nseg - 1)
    def _():
        s1 = jnp.zeros((tm, 1), F32)
        for j in range(nseg):
            s1 = s1 + jnp.sum(scr[nseg + j], axis=-1, keepdims=True)
        mu = s1 / d_a
        s2 = jnp.zeros((tm, 1), F32)
        for j in range(nseg):
            dlt = scr[nseg + j] - mu
            s2 = s2 + jnp.sum(dlt * dlt, axis=-1, keepdims=True)
        rstd = lax.rsqrt(s2 / d_a + EPS)
        for g in range(d_a // LANES):
            t_idx, sl = g // per, slice((g % per) * LANES, (g % per + 1) * LANES)
            gsl = slice(g * LANES, (g + 1) * LANES)
            vn = (scr[nseg + t_idx][:, sl] - mu) * rstd * gva_ref[:, gsl] + bva_ref[:, gsl]
            if emit_vn:
                vn_ref[:, gsl] = vn
            vb = vn.astype(BF16)
            u = scr[t_idx][:, sl]
            z = scr[2 * nseg + t_idx][:, sl]
            for c in range(tm // CHUNK):
                rs = slice(c * CHUNK, (c + 1) * CHUNK)
                mixed = jnp.dot(ws_ref[g], vb[rs], preferred_element_type=F32) + bs_ref[g]
                a_ref[rs, gsl] = (u[rs] * mixed * (z[rs] * _sigmoid(z[rs]))).astype(BF16)


def _branch_a(h, w_a, g_va, b_va, ws, bs, *, emit_vn, name):
    m, kdim = h.shape
    d_a = g_va.shape[0]
    tm = _pick(m, (512, 256, 128))
    tn = _pick(d_a, (512, 256, 128))
    ng = d_a // LANES
    out_shape = [jax.ShapeDtypeStruct((m, d_a), BF16)]
    out_specs = [pl.BlockSpec((tm, d_a), lambda i, j: (i, 0))]
    if emit_vn:
        out_shape.append(jax.ShapeDtypeStruct((m, d_a), F32))
        out_specs.append(pl.BlockSpec((tm, d_a), lambda i, j: (i, 0)))
    return pl.pallas_call(
        functools.partial(_branch_a_kernel, tm=tm, tn=tn, d_a=d_a, emit_vn=emit_vn),
        grid=(m // tm, 3 * d_a // tn),
        in_specs=[pl.BlockSpec((tm, kdim), lambda i, j: (i, 0)),
                  pl.BlockSpec((tn, kdim), lambda i, j: (j, 0)),
                  pl.BlockSpec((1, d_a), lambda i, j: (0, 0)),
                  pl.BlockSpec((1, d_a), lambda i, j: (0, 0)),
                  pl.BlockSpec((ng, CHUNK, CHUNK), lambda i, j: (0, 0, 0)),
                  pl.BlockSpec((ng, CHUNK, LANES), lambda i, j: (0, 0, 0))],
        out_specs=out_specs,
        out_shape=out_shape,
        scratch_shapes=[pltpu.VMEM((3 * d_a // tn, tm, tn), F32)],
        compiler_params=_cparams(("parallel", "arbitrary")),
        name=name,
    )(h, w_a, g_va.reshape(1, d_a), b_va.reshape(1, d_a), ws, bs)


def _sortable_key(score):
    bits = lax.bitcast_convert_type(score, I32)
    return jnp.where(bits < 0, (bits ^ 0x7FFFFFFF) + 1, bits)


def _kth_largest_key(count_ge, shape, topk):
    n_valid = count_ge(jnp.full(shape, INT_MIN + 1, I32))

    def unresolved(cnt):
        return (jnp.max(jnp.where(jnp.logical_and(cnt != topk, n_valid >= topk), 1.0, 0.0)) > 0.0).astype(I32)

    def cond(state):
        b, _, _, go = state
        return jnp.logical_and(b < 32, go > 0)

    def body(state):
        b, lo, cnt_lo, _ = state
        cand = lo + lax.shift_left(jnp.int32(1), jnp.int32(31) - b)
        cnt = count_ge(cand)
        take = cnt >= topk
        lo = jnp.where(take, cand, lo)
        cnt_lo = jnp.where(take, cnt, cnt_lo)
        return b + 1, lo, cnt_lo, unresolved(cnt_lo)

    lo0 = jnp.full(shape, INT_MIN, I32)
    cnt0 = jnp.full(shape, 2.0 ** 31, F32)
    _, thr, n_ge, _ = lax.while_loop(cond, body, (jnp.int32(0), lo0, cnt0, jnp.int32(1)))
    return thr, n_ge


def _tie_index_cut(count_eq_below, need, shape, nbits):
    def body(b, ans):
        cand = ans + lax.shift_left(jnp.int32(1), jnp.int32(nbits - 1) - b)
        return jnp.where(count_eq_below(cand) < need, cand, ans)
    return lax.fori_loop(0, nbits, body, jnp.zeros(shape, I32))


def _idx_kernel(qi_ref, wi_ref, ki_ref, mask_ref, key_scr, wb_scr, *, tq, kc, n_idx, n_chunks_total, topk, nbits):
    i = pl.program_id(0)
    nch = i + 1
    per = kc // LANES
    for h in range(n_idx):
        wb_scr[h] = jnp.broadcast_to(wi_ref[:, h:h + 1], (tq, LANES))
    q_pos = i * tq + lax.broadcasted_iota(I32, (1, tq), 1)
    sub = lax.broadcasted_iota(I32, (kc, 1), 0)

    def score_chunk(c, carry):
        kic = ki_ref[pl.ds(pl.multiple_of(c * kc, kc), kc), :]
        acc = [jnp.zeros((tq, LANES), F32) for _ in range(per)]
        for h in range(n_idx):
            d = lax.dot_general(qi_ref[:, h * LANES:(h + 1) * LANES], kic, _NT, preferred_element_type=F32)
            w = wb_scr[h]
            for j in range(per):
                acc[j] = acc[j] + jnp.maximum(d[:, j * LANES:(j + 1) * LANES], 0.0) * w
        score_t = jnp.concatenate(acc, axis=1).T
        key_scr[c] = jnp.where(c * kc + sub <= q_pos, _sortable_key(score_t), INT_MIN)
        return carry

    lax.fori_loop(0, nch, score_chunk, 0)

    def count(pred):
        def body(c, cnt):
            hit = jnp.where(pred(key_scr[c], c * kc + sub), 1.0, 0.0)
            return cnt + jnp.sum(hit.reshape(kc // 8, 8, tq), axis=0)
        cnt = lax.fori_loop(0, nch, body, jnp.zeros((8, tq), F32))
        return jnp.sum(cnt, axis=0, keepdims=True)

    def write_mask(pred):
        def body(c, carry):
            mask_ref[0, c] = jnp.where(pred(key_scr[c], c * kc + sub), 1.0, 0.0).astype(BF16)
            return carry
        lax.fori_loop(0, nch, body, 0)

    thr, n_ge = _kth_largest_key(lambda cand: count(lambda k, kidx: k >= cand), (1, tq), topk)
    tie = jnp.logical_and(n_ge > topk, thr > INT_MIN)
    thr_valid = jnp.maximum(thr, INT_MIN + 1)
    has_tie = jnp.max(jnp.where(tie, 1.0, 0.0)) > 0.0

    def write_fast():
        write_mask(lambda k, kidx: k >= thr_valid)

    def write_ties():
        n_gt = count(lambda k, kidx: k > thr)
        need = topk - n_gt
        cut = _tie_index_cut(lambda cand: count(lambda k, kidx: jnp.logical_and(k == thr, kidx < cand)),
                             need, (1, tq), nbits)
        cut = jnp.where(tie, cut, jnp.int32(2 ** 30))
        write_mask(lambda k, kidx: jnp.logical_and(k >= thr_valid, jnp.logical_or(k > thr, kidx <= cut)))

    lax.cond(has_tie, write_ties, write_fast)

    def fill(c, carry):
        mask_ref[0, c] = jnp.zeros((kc, tq), BF16)
        return carry
    lax.fori_loop(nch, n_chunks_total, fill, 0)


def _prompt_select(qi, wi, ki, *, n_idx, name):
    s = qi.shape[0]
    tq = kc = _pick(s, (256, 128))
    nq = s // tq
    topk = min(TOPK_MAX, s // 4)
    nbits = max(1, int(np.ceil(np.log2(s))))
    return pl.pallas_call(
        functools.partial(_idx_kernel, tq=tq, kc=kc, n_idx=n_idx, n_chunks_total=nq, topk=topk, nbits=nbits),
        grid=(nq,),
        in_specs=[pl.BlockSpec((tq, n_idx * LANES), lambda i: (i, 0)),
                  pl.BlockSpec((tq, LANES), lambda i: (i, 0)),
                  pl.BlockSpec((s, LANES), lambda i: (0, 0))],
        out_specs=pl.BlockSpec((1, nq, kc, tq), lambda i: (i, 0, 0, 0)),
        out_shape=jax.ShapeDtypeStruct((nq, nq, kc, tq), BF16),
        scratch_shapes=[pltpu.VMEM((nq, kc, tq), I32), pltpu.VMEM((n_idx, tq, LANES), F32)],
        compiler_params=_cparams(("parallel",)),
        name=name,
    )(qi, wi, ki)


def _attn_kernel(shift_ref, q_ref, k_ref, vt_ref, mask_ref, o_ref, m_scr, acc_scr, *, rq, rk, kc, hps, scale2,
                 fixed_shift):
    i = pl.program_id(1)
    acc_scr[...] = jnp.zeros(acc_scr.shape, F32)
    if not fixed_shift:
        m_scr[...] = jnp.full(m_scr.shape, NEG, F32)
    ones = jnp.ones((acc_scr.shape[1] - LANES, kc), BF16)
    shift = shift_ref[0]

    def body(c, carry):
        keep = jnp.concatenate(
            [jnp.concatenate([mask_ref[a, c * rk + b] for a in range(rq)], axis=1) for b in range(rk)], axis=0)
        rows = pl.ds(pl.multiple_of(c * kc, kc), kc)
        heads = [slice(hh * LANES, (hh + 1) * LANES) for hh in range(hps)]
        logits = [lax.dot_general(k_ref[rows, sl], q_ref[:, sl], _NT, preferred_element_type=F32) for sl in heads]
        for hh, sl in enumerate(heads):
            v_ext = jnp.concatenate([vt_ref[c, sl, :], ones], axis=0)
            if fixed_shift:
                p = jnp.exp2(logits[hh] * scale2 - shift).astype(BF16) * keep
                acc_scr[hh] = acc_scr[hh] + jnp.dot(v_ext, p, preferred_element_type=F32)
            else:
                s = logits[hh] * scale2 + (keep.astype(F32) - 1.0) * (-NEG)
                m = m_scr[hh]
                m_new = jnp.maximum(m, jnp.max(s, axis=0, keepdims=True))
                alpha = jnp.exp2(m - m_new)
                p = jnp.exp2(s - m_new).astype(BF16)
                acc_scr[hh] = alpha * acc_scr[hh] + jnp.dot(v_ext, p, preferred_element_type=F32)
                m_scr[hh] = m_new
        return carry

    lax.fori_loop(0, i + 1, body, 0)
    for hh in range(hps):
        acc = acc_scr[hh]
        o_ref[:, hh * LANES:(hh + 1) * LANES] = (acc[:LANES] / acc[LANES:LANES + 1]).T.astype(o_ref.dtype)


_FIXED_SHIFT_MAX = 50.0


def _prompt_attention(q, k, vt, mask, g_q, g_k, *, name):
    s, d_b = q.shape
    nqs, nks, kcs, tqs = mask.shape
    nkc, _, kc = vt.shape
    tq = kc
    rq, rk = tq // tqs, kc // kcs
    hps = _pick(d_b // LANES, (4, 2, 1))
    wblk = hps * LANES
    once = pl.Buffered(1)
    scale2 = float(LANES ** -0.5 * np.log2(np.e))
    bound = (1.01 * LANES * scale2) * jnp.max(jnp.abs(g_q)) * jnp.max(jnp.abs(g_k))

    def run(fixed_shift):
        return pl.pallas_call(
            functools.partial(_attn_kernel, rq=rq, rk=rk, kc=kc, hps=hps, scale2=scale2, fixed_shift=fixed_shift),
            grid=(d_b // wblk, s // tq),
            in_specs=[pl.BlockSpec(memory_space=pltpu.SMEM),
                      pl.BlockSpec((tq, wblk), lambda h, i: (i, h)),
                      pl.BlockSpec((s, wblk), lambda h, i: (0, h), pipeline_mode=once),
                      pl.BlockSpec((nkc, wblk, kc), lambda h, i: (0, h, 0), pipeline_mode=once),
                      pl.BlockSpec((rq, nks, kcs, tqs), lambda h, i: (i, 0, 0, 0))],
            out_specs=pl.BlockSpec((tq, wblk), lambda h, i: (i, h)),
            out_shape=jax.ShapeDtypeStruct((s, d_b), BF16),
            scratch_shapes=[pltpu.VMEM((hps, 1, tq), F32), pltpu.VMEM((hps, LANES + 16, tq), F32)],
            compiler_params=_cparams(("parallel", "arbitrary")),
            name=name + ("_fixed_shift" if fixed_shift else "_online"),
        )(bound.reshape(1).astype(F32), q, k, vt, mask)

    return lax.cond(bound <= _FIXED_SHIFT_MAX, lambda: run(True), lambda: run(False))


def _sample_idx_kernel(pt_ref, *refs, g_pages, n_steps, n_idx, dec_seq, topk, nbits, past, att_keys):
    del pt_ref
    pages = refs[:g_pages]
    qi_ref, wcol_ref, kinew_ref, bias_past_ref, bias_new_ref, kb_scr, key_scr, keyn_scr = refs[g_pages:]
    j = pl.program_id(1)
    pg = g_pages * PAGE_SIZE
    rows = 8
    row_id = lax.broadcasted_iota(I32, (rows, 1), 0)

    def scores_of(kmat):
        d = lax.dot_general(qi_ref[0], kmat, _NT, preferred_element_type=F32)
        wr = jnp.maximum(d, 0.0) * wcol_ref[0]
        s4 = jnp.sum(wr.reshape(dec_seq, n_idx, kmat.shape[0]), axis=1)
        return jnp.concatenate([s4, jnp.zeros((rows - dec_seq, kmat.shape[0]), F32)], axis=0)

    for r in range(g_pages):
        kb_scr[r * PAGE_SIZE:(r + 1) * PAGE_SIZE, :] = pages[r][0].astype(BF16)
    key_scr[j] = jnp.where(row_id < dec_seq, _sortable_key(scores_of(kb_scr[...])), INT_MIN)

    @pl.when(j == n_steps - 1)
    def _():
        sn = scores_of(kinew_ref[0])
        coln = lax.broadcasted_iota(I32, (1, LANES), 1)
        okn = jnp.logical_and(jnp.logical_and(coln < dec_seq, coln <= row_id), row_id < dec_seq)
        keyn_scr[...] = jnp.where(okn, _sortable_key(sn), INT_MIN)

        def count(pred):
            cnt = jnp.zeros((rows, LANES), F32)
            for c in range(n_steps):
                p = pred(key_scr[c], c * pg + lax.broadcasted_iota(I32, (1, pg), 1))
                for jj in range(pg // LANES):
                    cnt = cnt + jnp.where(p[:, jj * LANES:(jj + 1) * LANES], 1.0, 0.0)
            cnt = cnt + jnp.where(pred(keyn_scr[...], past + coln), 1.0, 0.0)
            return jnp.sum(cnt, axis=-1, keepdims=True)

        thr, n_ge = _kth_largest_key(lambda cand: count(lambda k, col: k >= cand), (rows, 1), topk)
        n_gt = count(lambda k, col: k > thr)
        tie = jnp.logical_and(n_ge > topk, thr > INT_MIN)
        thr_valid = jnp.maximum(thr, INT_MIN + 1)
        need = topk - n_gt
        cut = _tie_index_cut(
            lambda cand: count(lambda k, col: jnp.logical_and(k == thr, col < cand)), need, (rows, 1), nbits)
        cut = jnp.where(tie, cut, jnp.int32(2 ** 30))

        def sel(k, col):
            return jnp.logical_and(k >= thr_valid, jnp.logical_or(k > thr, col <= cut))

        n_sub = pg // att_keys
        for c in range(n_steps):
            b = jnp.where(sel(key_scr[c], c * pg + lax.broadcasted_iota(I32, (1, pg), 1)), 0.0, NEG)
            for u in range(n_sub):
                bias_past_ref[0, n_sub * c + u] = b[:, u * att_keys:(u + 1) * att_keys]
        bias_new_ref[0] = jnp.where(sel(keyn_scr[...], past + coln), 0.0, NEG)


def _sample_select(page_table, cache_kidx, qi_rows, wcol, kinew, *, n_idx, dec_seq, name):
    bd, n_pages = page_table.shape
    g_pages = _pick(n_pages, (16, 8, 4, 2, 1))
    n_steps = n_pages // g_pages
    pg = g_pages * PAGE_SIZE
    att_keys = _pick(pg, (8 * PAGE_SIZE, 4 * PAGE_SIZE, 2 * PAGE_SIZE, PAGE_SIZE))
    n_att = n_steps * (pg // att_keys)
    past = n_pages * PAGE_SIZE
    topk = min(TOPK_MAX, (past + dec_seq) // 4)
    nbits = max(1, int(np.ceil(np.log2(past + LANES))))
    rq = dec_seq * n_idx

    def page_spec(r):
        return pl.BlockSpec((1, PAGE_SIZE, LANES), lambda b, j, pt: (pt[b, j * g_pages + r], 0, 0))

    grid_spec = pltpu.PrefetchScalarGridSpec(
        num_scalar_prefetch=1,
        grid=(bd, n_steps),
        in_specs=[page_spec(r) for r in range(g_pages)] + [
            pl.BlockSpec((1, rq, LANES), lambda b, j, pt: (b, 0, 0)),
            pl.BlockSpec((1, rq, 1), lambda b, j, pt: (b, 0, 0)),
            pl.BlockSpec((1, LANES, LANES), lambda b, j, pt: (b, 0, 0))],
        out_specs=[pl.BlockSpec((1, n_att, 8, att_keys), lambda b, j, pt: (b, 0, 0, 0)),
                   pl.BlockSpec((1, 8, LANES), lambda b, j, pt: (b, 0, 0))],
        scratch_shapes=[pltpu.VMEM((pg, LANES), BF16), pltpu.VMEM((n_steps, 8, pg), I32),
                        pltpu.VMEM((8, LANES), I32)],
    )
    return pl.pallas_call(
        functools.partial(_sample_idx_kernel, g_pages=g_pages, n_steps=n_steps, n_idx=n_idx,
                          dec_seq=dec_seq, topk=topk, nbits=nbits, past=past, att_keys=att_keys),
        grid_spec=grid_spec,
        out_shape=[jax.ShapeDtypeStruct((bd, n_att, 8, att_keys), F32),
                   jax.ShapeDtypeStruct((bd, 8, LANES), F32)],
        compiler_params=_cparams(("parallel", "arbitrary")),
        name=name,
    )(page_table, *([cache_kidx] * g_pages), qi_rows, wcol, kinew)


def _sample_attn_kernel(pt_ref, *refs, g_pages, n_steps, n_heads, scale):
    del pt_ref
    kpages = refs[:g_pages]
    vpages = refs[g_pages:2 * g_pages]
    (qblk_ref, bias_ref, biasn_ref, knew_ref, vnew_ref, o_ref,
     kb_scr, vb_scr, m_scr, l_scr, acc_scr) = refs[2 * g_pages:]
    j = pl.program_id(1)
    rows = 8 * n_heads

    @pl.when(j == 0)
    def _():
        m_scr[...] = jnp.full(m_scr.shape, NEG, F32)
        l_scr[...] = jnp.zeros(l_scr.shape, F32)
        acc_scr[...] = jnp.zeros(acc_scr.shape, F32)

    def logits_t(kb):
        return jnp.dot(kb, qblk_ref[0], preferred_element_type=F32)

    def update(st, vb, bias8):
        s = st.T * scale + jnp.concatenate([bias8] * n_heads, axis=0)
        m = m_scr[...]
        m_new = jnp.maximum(m, jnp.max(s, axis=-1, keepdims=True))
        alpha = jnp.exp2(m - m_new)
        p = jnp.exp2(s - m_new)
        l_scr[...] = alpha * l_scr[...] + jnp.sum(p, axis=-1, keepdims=True)
        acc_scr[...] = alpha * acc_scr[...] + jnp.dot(p.astype(BF16), vb, preferred_element_type=F32)
        m_scr[...] = m_new

    def gather_heads(pages, dst):
        kg = 16
        for r in range(g_pages):
            for g in range(PAGE_SIZE // kg):
                blk = pages[r][0, 0, g * kg * n_heads:(g + 1) * kg * n_heads, :].astype(BF16)
                by_head = pltpu.einshape("khd->hkd", blk.reshape(kg, n_heads, LANES))
                ks = slice(r * PAGE_SIZE + g * kg, r * PAGE_SIZE + (g + 1) * kg)
                for h in range(n_heads):
                    dst[ks, h * LANES:(h + 1) * LANES] = by_head[h]

    gather_heads(kpages, kb_scr)
    st = logits_t(kb_scr[...])
    gather_heads(vpages, vb_scr)
    update(st, vb_scr[...], bias_ref[0, 0])

    @pl.when(j == n_steps - 1)
    def _():
        update(logits_t(knew_ref[0]), vnew_ref[0], biasn_ref[0])
        for h in range(n_heads):
            rs = slice(h * 8, (h + 1) * 8)
            cs = slice(h * LANES, (h + 1) * LANES)
            o_ref[0, :, cs] = acc_scr[rs, cs] / l_scr[rs, :]


def _sample_attention(page_table, cache_k, cache_v, layer, qblk, bias_past, bias_new, knew, vnew, *, n_heads, name):
    bd, n_pages = page_table.shape
    d_b = n_heads * LANES
    n_steps = bias_past.shape[1]
    g_pages = n_pages // n_steps
    pg = g_pages * PAGE_SIZE
    rows = 8 * n_heads

    def page_spec(r):
        return pl.BlockSpec((1, 1, PAGE_SIZE * n_heads, LANES),
                            lambda b, j, pt: (layer, pt[b, j * g_pages + r], 0, 0))

    grid_spec = pltpu.PrefetchScalarGridSpec(
        num_scalar_prefetch=1,
        grid=(bd, n_steps),
        in_specs=[page_spec(r) for r in range(g_pages)] * 2 + [
            pl.BlockSpec((1, d_b, rows), lambda b, j, pt: (b, 0, 0)),
            pl.BlockSpec((1, 1, 8, pg), lambda b, j, pt: (b, j, 0, 0)),
            pl.BlockSpec((1, 8, LANES), lambda b, j, pt: (b, 0, 0)),
            pl.BlockSpec((1, LANES, d_b), lambda b, j, pt: (b, 0, 0)),
            pl.BlockSpec((1, LANES, d_b), lambda b, j, pt: (b, 0, 0))],
        out_specs=pl.BlockSpec((1, 8, d_b), lambda b, j, pt: (b, 0, 0)),
        scratch_shapes=[pltpu.VMEM((pg, d_b), BF16), pltpu.VMEM((pg, d_b), BF16),
                        pltpu.VMEM((rows, 1), F32), pltpu.VMEM((rows, 1), F32), pltpu.VMEM((rows, d_b), F32)],
    )
    return pl.pallas_call(
        functools.partial(_sample_attn_kernel, g_pages=g_pages, n_steps=n_steps, n_heads=n_heads,
                          scale=float(LANES ** -0.5 * np.log2(np.e))),
        grid_spec=grid_spec,
        out_shape=jax.ShapeDtypeStruct((bd, 8, d_b), F32),
        compiler_params=_cparams(("parallel", "arbitrary")),
        name=name,
    )(page_table, *([cache_k] * g_pages), *([cache_v] * g_pages), qblk, bias_past, bias_new, knew, vnew)


def _merge_kernel(h_ref, a_ref, attn_ref, zs_ref, wga_ref, wgb_ref, wua_ref, wub_ref, o_ref, bg_scr):
    @pl.when(pl.program_id(1) == 0)
    def _():
        bg_scr[...] = (attn_ref[...].astype(F32) * zs_ref[...].astype(F32)).astype(BF16)

    h = h_ref[...]
    ga = lax.dot_general(h, wga_ref[...], _NT, preferred_element_type=F32)
    gb = lax.dot_general(h, wgb_ref[...], _NT, preferred_element_type=F32)
    a_up = jnp.dot(a_ref[...], wua_ref[...].astype(BF16), preferred_element_type=F32)
    b_up = jnp.dot(bg_scr[...], wub_ref[...].astype(BF16), preferred_element_type=F32)
    o_ref[...] = (_sigmoid(ga) * a_up + _sigmoid(gb) * b_up).astype(o_ref.dtype)


def _merge(h, a_gated, attn, zs, w_ga, w_gb, w_ua, w_ub, *, name):
    m, d = h.shape
    d_a, d_b = a_gated.shape[1], attn.shape[1]
    tm = _pick(m, (512, 256, 128))
    tn = _pick(d, (256, 128))
    row = lambda i, j: (i, 0)
    col = lambda i, j: (0, j)
    return pl.pallas_call(
        _merge_kernel,
        grid=(m // tm, d // tn),
        in_specs=[pl.BlockSpec((tm, d), row), pl.BlockSpec((tm, d_a), row),
                  pl.BlockSpec((tm, d_b), row), pl.BlockSpec((tm, d_b), row),
                  pl.BlockSpec((tn, d), lambda i, j: (j, 0)), pl.BlockSpec((tn, d), lambda i, j: (j, 0)),
                  pl.BlockSpec((d_a, tn), col), pl.BlockSpec((d_b, tn), col)],
        out_specs=pl.BlockSpec((tm, tn), lambda i, j: (i, j)),
        out_shape=jax.ShapeDtypeStruct((m, d), BF16),
        scratch_shapes=[pltpu.VMEM((tm, d_b), BF16)],
        compiler_params=_cparams(("parallel", "arbitrary")),
        name=name,
    )(h, a_gated, attn, zs, w_ga, w_gb, w_ua, w_ub)


def _resid_kernel(mg_ref, w_ref, x_ref, o_ref):
    o_ref[...] = x_ref[...] + jnp.dot(mg_ref[...], w_ref[...].astype(BF16), preferred_element_type=F32)


def _resid_proj(merged, w_o, x, *, name):
    m, d = x.shape
    tm = _pick(m, (1024, 512, 256, 128))
    tn = _pick(d, (512, 256, 128))
    return pl.pallas_call(
        _resid_kernel,
        grid=(m // tm, d // tn),
        in_specs=[pl.BlockSpec((tm, merged.shape[1]), lambda i, j: (i, 0)),
                  pl.BlockSpec((merged.shape[1], tn), lambda i, j: (0, j)),
                  pl.BlockSpec((tm, tn), lambda i, j: (i, j))],
        out_specs=pl.BlockSpec((tm, tn), lambda i, j: (i, j)),
        out_shape=jax.ShapeDtypeStruct((m, d), F32),
        compiler_params=_cparams(("parallel", "arbitrary")),
        name=name,
    )(merged, w_o, x)


def _ple_kernel(hn_ref, wg_ref, p_ref, wp_ref, x1_ref, o_ref):
    gate = _sigmoid(jnp.dot(hn_ref[...], wg_ref[...].astype(BF16), preferred_element_type=F32))
    emb = jnp.dot(p_ref[...], wp_ref[...].astype(BF16), preferred_element_type=F32)
    o_ref[...] = x1_ref[...] + gate * emb


def _ple(hn, w_gate, p, w_proj, x1, *, name):
    m, d = x1.shape
    tm = _pick(m, (1024, 512, 256, 128))
    tn = _pick(d, (512, 256, 128))
    return pl.pallas_call(
        _ple_kernel,
        grid=(m // tm, d // tn),
        in_specs=[pl.BlockSpec((tm, d), lambda i, j: (i, 0)),
                  pl.BlockSpec((d, tn), lambda i, j: (0, j)),
                  pl.BlockSpec((tm, p.shape[1]), lambda i, j: (i, 0)),
                  pl.BlockSpec((p.shape[1], tn), lambda i, j: (0, j)),
                  pl.BlockSpec((tm, tn), lambda i, j: (i, j))],
        out_specs=pl.BlockSpec((tm, tn), lambda i, j: (i, j)),
        out_shape=jax.ShapeDtypeStruct((m, d), F32),
        compiler_params=_cparams(("parallel", "arbitrary")),
        name=name,
    )(hn, w_gate, p, w_proj, x1)


def _rope_tables(pos):
    half = LANES // 2
    inv = ROPE_THETA ** (-jnp.arange(half, dtype=F32) / half)
    ang = pos.astype(F32)[:, None] * inv[None, :]
    cos, sin = jnp.cos(ang), jnp.sin(ang)
    return jnp.concatenate([cos, cos], axis=-1), jnp.concatenate([-sin, sin], axis=-1)


def _input_weights(w_in, d_model, d_a, d_b, n_idx):
    widths = [3 * d_a, d_b, d_b, d_b, d_b, n_idx * LANES, LANES + n_idx, d_model, d_model]
    offs = [int(v) for v in np.concatenate([[0], np.cumsum(widths)])]
    w_t = jnp.swapaxes(w_in, 0, 1).astype(BF16)
    cols = dict(zip(("a", "q", "k", "v", "zb", "qi", "kiwi"), offs[:7]))
    return w_t, cols, w_t[offs[7]:offs[8]], w_t[offs[8]:offs[9]]


def _layer(x, p, pos, tag, w, ws, bs, *, n_heads, n_idx, emit_vn, v_chunk=0):
    m, d_model = x.shape
    cosf, sinf = _rope_tables(pos)
    h = _rmsnorm(x, w["g_norm"], f"{tag}_norm")
    w_all, cols, d_b = w["w_all"], w["cols"], n_heads * LANES
    a_out = _branch_a(h, w_all, w["g_va"], w["b_va"], ws, bs, emit_vn=emit_vn, name=f"{tag}_branch_a")
    gq = jnp.tile(w["g_q"], n_heads).reshape(1, -1)
    gk = jnp.tile(w["g_k"], n_heads).reshape(1, -1)
    (q_bf,) = _proj(h, w_all, col_off=cols["q"], n=d_b, kind="norm_rope", n_f32=0, n_bf16=1, gain=gq, cos=cosf,
                    sin=sinf, name=f"{tag}_q")
    k_f32, k_bf = _proj(h, w_all, col_off=cols["k"], n=d_b, kind="norm_rope", n_f32=1, n_bf16=1, gain=gk, cos=cosf,
                        sin=sinf, name=f"{tag}_k")
    if v_chunk:
        v_f32, v_bf = _proj(h, w_all, col_off=cols["v"], n=d_b, kind="plain", n_f32=1, n_bf16=0, t_chunk=v_chunk,
                            name=f"{tag}_v")
    else:
        v_f32, v_bf = _proj(h, w_all, col_off=cols["v"], n=d_b, kind="plain", n_f32=1, n_bf16=1, name=f"{tag}_v")
    (zs,) = _proj(h, w_all, col_off=cols["zb"], n=d_b, kind="silu", n_f32=0, n_bf16=1, name=f"{tag}_zb")
    (qi,) = _proj(h, w_all, col_off=cols["qi"], n=n_idx * LANES, kind="rope", n_f32=0, n_bf16=1, cos=cosf, sin=sinf,
                  name=f"{tag}_qi")
    ki_f32, ki_bf, wi = _proj(h, w_all, col_off=cols["kiwi"], n=2 * LANES, kind="kiwi", n_f32=0, n_bf16=0, cos=cosf,
                              sin=sinf, scale=float(n_idx ** -0.5 * LANES ** -0.5), name=f"{tag}_kiwi")
    return dict(h=h, a=a_out, q=q_bf, k=k_f32, kb=k_bf, v=v_f32, vb=v_bf, zs=zs, qi=qi, ki=ki_f32, kib=ki_bf, wi=wi)


def _finish(x, p, t, attn, tag, w):
    merged = _merge(t["h"], t["a"][0], attn, t["zs"], w["w_ga"], w["w_gb"], w["w_ua"], w["w_ub"], name=f"{tag}_merge")
    x1 = _resid_proj(merged, w["w_o"], x, name=f"{tag}_resid")
    hn = _rmsnorm(x1, w["g_ple"], f"{tag}_ple_norm")
    return _ple(hn, w["w_pg"], p.astype(BF16), w["w_pp"], x1, name=f"{tag}_ple")


def kernel(x_prompt, x_sample, p_prompt, p_sample, cache_k, cache_v, cache_kidx, page_table, g_norm, w_in, g_va,
           b_va, w_s, b_s, g_q, g_k, w_up_a, w_up_b, w_o, g_ple, w_ple_gate, w_ple_proj):
    depth = w_in.shape[0]
    batch, seq, d_model = x_prompt.shape
    bd, dec_seq, _ = x_sample.shape
    n_phys, _, n_heads, head_dim = cache_k.shape[1:]
    n_groups = w_s.shape[1]
    d_a, d_b = w_up_a.shape[1], w_up_b.shape[1]
    n_idx = w_in.shape[2] - (3 * d_a + 4 * d_b + LANES + 2 * d_model)
    n_idx = n_idx // (LANES + 1)
    n_pages = page_table.shape[1]
    past = n_pages * PAGE_SIZE
    assert batch == 1 and head_dim == LANES and cache_kidx.shape[-1] == LANES and d_a == n_groups * LANES
    assert seq % CHUNK == 0 and bd * dec_seq == CHUNK and dec_seq <= 8 and d_b == n_heads * LANES
    ms = bd * dec_seq

    yp = x_prompt.reshape(seq, d_model)
    ys = x_sample.reshape(ms, d_model)
    pos_p = jnp.arange(seq)
    pos_s = jnp.tile(past + jnp.arange(dec_seq), bd)
    outs = {k: [] for k in ("kp", "vp", "kip", "ks", "vs", "kis", "vas")}
    tril = jnp.tril(jnp.ones((CHUNK, CHUNK), bool))
    cache_k2 = cache_k.reshape(depth, n_phys, PAGE_SIZE * n_heads, LANES)
    cache_v2 = cache_v.reshape(depth, n_phys, PAGE_SIZE * n_heads, LANES)
    for i in range(depth):
        w_all, cols, w_ga, w_gb = _input_weights(w_in[i], d_model, d_a, d_b, n_idx)
        w = dict(w_all=w_all, cols=cols, w_ga=w_ga, w_gb=w_gb)
        w.update(g_norm=g_norm[i], g_va=g_va[i], b_va=b_va[i], g_q=g_q[i], g_k=g_k[i], g_ple=g_ple[i],
                 w_ua=w_up_a[i], w_ub=w_up_b[i], w_o=w_o[i], w_pg=w_ple_gate[i], w_pp=w_ple_proj[i])
        ws_p = jnp.where(tril[None], w_s[i], 0.0).astype(BF16)
        bs_p = jnp.broadcast_to(b_s[i][:, :, None], (n_groups, CHUNK, LANES))
        small = jnp.where(tril[None, :dec_seq, :dec_seq], w_s[i][:, :dec_seq, :dec_seq], 0.0)
        ws_s = jnp.einsum("ab,gts->gatbs", jnp.eye(bd, dtype=F32), small).reshape(n_groups, ms, ms).astype(BF16)
        bs_s = jnp.broadcast_to(jnp.tile(b_s[i][:, :dec_seq], (1, bd))[:, :, None], (n_groups, ms, LANES))

        tp = _layer(yp, p_prompt[i, 0], pos_p, "prompt", w, ws_p, bs_p, n_heads=n_heads, n_idx=n_idx, emit_vn=False,
                    v_chunk=_pick(seq, (512, 256, 128)))
        mask = _prompt_select(tp["qi"], tp["wi"], tp["kib"], n_idx=n_idx, name="prompt_select")
        attn_p = _prompt_attention(tp["q"], tp["kb"], tp["vb"], mask, g_q[i], g_k[i], name="prompt_attention")
        yp_new = _finish(yp, p_prompt[i, 0], tp, attn_p, "prompt", w)

        tsm = _layer(ys, p_sample[i].reshape(ms, -1), pos_s, "sample", w, ws_s, bs_s, n_heads=n_heads, n_idx=n_idx,
                     emit_vn=True)
        qi_rows = tsm["qi"].reshape(bd, dec_seq * n_idx, LANES)
        wcol = tsm["wi"][:, :n_idx].reshape(bd, dec_seq * n_idx, 1)
        kinew = jnp.pad(tsm["kib"].reshape(bd, dec_seq, LANES), ((0, 0), (0, LANES - dec_seq), (0, 0)))
        bias_past, bias_new = _sample_select(page_table, cache_kidx[i], qi_rows, wcol, kinew, n_idx=n_idx,
                                             dec_seq=dec_seq, name="sample_select")
        q4 = jnp.pad(tsm["q"].reshape(bd, dec_seq, n_heads, LANES), ((0, 0), (0, 8 - dec_seq), (0, 0), (0, 0)))
        qblk = jnp.einsum("bthd,hg->bhdgt", q4, jnp.eye(n_heads, dtype=BF16)).reshape(bd, d_b, n_heads * 8)
        padrows = ((0, 0), (0, LANES - dec_seq), (0, 0))
        knew = jnp.pad(tsm["kb"].reshape(bd, dec_seq, d_b), padrows)
        vnew = jnp.pad(tsm["vb"].reshape(bd, dec_seq, d_b), padrows)
        attn_s8 = _sample_attention(page_table, cache_k2, cache_v2, i, qblk, bias_past, bias_new,
                                    knew, vnew, n_heads=n_heads, name="sample_attention")
        attn_s = attn_s8[:, :dec_seq].reshape(ms, d_b).astype(BF16)
        ys_new = _finish(ys, p_sample[i].reshape(ms, -1), tsm, attn_s, "sample", w)

        outs["kp"].append(tp["k"].reshape(batch, seq, n_heads, LANES))
        outs["vp"].append(tp["v"].reshape(batch, seq, n_heads, LANES))
        outs["kip"].append(tp["ki"].reshape(batch, seq, LANES))
        outs["ks"].append(tsm["k"].reshape(bd, dec_seq, n_heads, LANES))
        outs["vs"].append(tsm["v"].reshape(bd, dec_seq, n_heads, LANES))
        outs["kis"].append(tsm["ki"].reshape(bd, dec_seq, LANES))
        outs["vas"].append(tsm["a"][1].reshape(bd, dec_seq, d_a))
        yp, ys = yp_new, ys_new

    return (yp.reshape(batch, seq, d_model), ys.reshape(bd, dec_seq, d_model),
            jnp.stack(outs["kp"]), jnp.stack(outs["vp"]), jnp.stack(outs["kip"]),
            jnp.stack(outs["ks"]), jnp.stack(outs["vs"]), jnp.stack(outs["kis"]), jnp.stack(outs["vas"]))
```

```python
import functools

import jax
import jax.numpy as jnp
import numpy as np
from jax import lax
from jax.experimental import pallas as pl
from jax.experimental.pallas import tpu as pltpu

F32 = jnp.float32
BF16 = jnp.bfloat16
I32 = jnp.int32

LANES = 128
PAGE_SIZE = 128
CHUNK = 128
TOPK_MAX = 256
ROPE_THETA = 10000.0
EPS = 1e-6
NEG = -1e30
INT_MIN = -(2 ** 31)
VMEM_LIMIT = 56 * 1024 * 1024

_NT = (((1,), (1,)), ((), ()))


def _cparams(sem):
    return pltpu.CompilerParams(dimension_semantics=sem, vmem_limit_bytes=VMEM_LIMIT)


def _sigmoid(x):
    return 1.0 / (1.0 + jnp.exp(-x))


def _pick(n, cands):
    for c in cands:
        if n % c == 0:
            return c
    return n


def _rmsnorm_kernel(x_ref, g_ref, o_ref):
    x = x_ref[...]
    ms = jnp.mean(x * x, axis=-1, keepdims=True)
    o_ref[...] = (x * lax.rsqrt(ms + EPS) * g_ref[...]).astype(o_ref.dtype)


def _rmsnorm(x, g, name):
    m, d = x.shape
    tm = _pick(m, (256, 128))
    return pl.pallas_call(
        _rmsnorm_kernel,
        grid=(m // tm,),
        in_specs=[pl.BlockSpec((tm, d), lambda i: (i, 0)),
                  pl.BlockSpec((1, d), lambda i: (0, 0))],
        out_specs=pl.BlockSpec((tm, d), lambda i: (i, 0)),
        out_shape=jax.ShapeDtypeStruct((m, d), BF16),
        compiler_params=_cparams(("parallel",)),
        name=name,
    )(x, g.reshape(1, d))


def _rope(x, cos, sin_signed):
    return x * cos + pltpu.roll(x, LANES // 2, axis=1) * sin_signed


def _proj_kernel(*refs, kind, n_f32, n_bf16, tn, scale, t_chunk):
    h_ref, w_ref = refs[0], refs[1]
    acc = lax.dot_general(h_ref[...], w_ref[...], _NT, preferred_element_type=F32)
    if t_chunk:
        t_ref, refs = refs[-1], refs[:-1]
        for a in range(acc.shape[0] // t_chunk):
            t_ref[a] = acc[a * t_chunk:(a + 1) * t_chunk, :].T.astype(BF16)
    pos = 2
    if kind == "norm_rope":
        g_ref, cos_ref, sin_ref = refs[pos:pos + 3]
        pos += 3
    elif kind in ("rope", "kiwi"):
        cos_ref, sin_ref = refs[pos:pos + 2]
        pos += 2
    outs = refs[pos:]
    if kind == "kiwi":
        ki = _rope(acc[:, :LANES], cos_ref[...], sin_ref[...])
        outs[0][...] = ki
        outs[1][...] = ki.astype(BF16)
        outs[2][...] = acc[:, LANES:] * scale
        return
    for j in range(tn // LANES):
        sl = slice(j * LANES, (j + 1) * LANES)
        x = acc[:, sl]
        if kind == "norm_rope":
            ms = jnp.mean(x * x, axis=-1, keepdims=True)
            x = x * lax.rsqrt(ms + EPS) * g_ref[:, sl]
            x = _rope(x, cos_ref[...], sin_ref[...])
        elif kind == "rope":
            x = _rope(x, cos_ref[...], sin_ref[...])
        elif kind == "silu":
            x = x * _sigmoid(x)
        k = 0
        for _ in range(n_f32):
            outs[k][:, sl] = x
            k += 1
        for _ in range(n_bf16):
            outs[k][:, sl] = x.astype(BF16)
            k += 1


def _proj(h, w, *, kind, n_f32, n_bf16, name, col_off, n, gain=None, cos=None, sin=None, scale=1.0, t_chunk=0):
    m, kdim = h.shape
    tm = _pick(m, (1024, 512, 256, 128))
    assert t_chunk == 0 or tm % t_chunk == 0
    tn = 256 if kind == "kiwi" else _pick(np.gcd(n, col_off) if col_off else n, (512, 256, 128))
    assert col_off % tn == 0 and n % tn == 0
    blk_off = col_off // tn
    in_specs = [pl.BlockSpec((tm, kdim), lambda i, j: (i, 0)),
                pl.BlockSpec((tn, kdim), lambda i, j: (j + blk_off, 0))]
    args = [h, w]
    if kind == "norm_rope":
        in_specs.append(pl.BlockSpec((1, tn), lambda i, j: (0, j)))
        args.append(gain)
    if kind in ("norm_rope", "rope", "kiwi"):
        in_specs += [pl.BlockSpec((tm, LANES), lambda i, j: (i, 0))] * 2
        args += [cos, sin]
    if kind == "kiwi":
        out_shape = [jax.ShapeDtypeStruct((m, LANES), F32), jax.ShapeDtypeStruct((m, LANES), BF16),
                     jax.ShapeDtypeStruct((m, LANES), F32)]
        out_specs = [pl.BlockSpec((tm, LANES), lambda i, j: (i, 0))] * 3
    else:
        out_shape = ([jax.ShapeDtypeStruct((m, n), F32)] * n_f32 + [jax.ShapeDtypeStruct((m, n), BF16)] * n_bf16)
        out_specs = [pl.BlockSpec((tm, tn), lambda i, j: (i, j))] * (n_f32 + n_bf16)
    if t_chunk:
        out_shape = out_shape + [jax.ShapeDtypeStruct((m // t_chunk, n, t_chunk), BF16)]
        out_specs = out_specs + [pl.BlockSpec((tm // t_chunk, tn, t_chunk), lambda i, j: (i, j, 0))]
    return pl.pallas_call(
        functools.partial(_proj_kernel, kind=kind, n_f32=n_f32, n_bf16=n_bf16, tn=tn, scale=scale,
                          t_chunk=t_chunk),
        grid=(m // tm, n // tn),
        in_specs=in_specs,
        out_specs=out_specs,
        out_shape=out_shape,
        compiler_params=_cparams(("parallel", "arbitrary")),
        name=name,
    )(*args)


def _branch_a_kernel(h_ref, w_ref, gva_ref, bva_ref, ws_ref, bs_ref, *rest, tm, tn, d_a, emit_vn):
    if emit_vn:
        a_ref, vn_ref, scr = rest
    else:
        a_ref, scr = rest
    n = pl.program_id(1)
    nseg = d_a // tn
    per = tn // LANES
    scr[n] = lax.dot_general(h_ref[...], w_ref[...], _NT, preferred_element_type=F32)

    @pl.when(n == 3 * nseg - 1)
    def _():
        s1 = jnp.zeros((tm, 1), F32)
        for j in range(nseg):
            s1 = s1 + jnp.sum(scr[nseg + j], axis=-1, keepdims=True)
        mu = s1 / d_a
        s2 = jnp.zeros((tm, 1), F32)
        for j in range(nseg):
            dlt = scr[nseg + j] - mu
            s2 = s2 + jnp.sum(dlt * dlt, axis=-1, keepdims=True)
        rstd = lax.rsqrt(s2 / d_a + EPS)
        for g in range(d_a // LANES):
            t_idx, sl = g // per, slice((g % per) * LANES, (g % per + 1) * LANES)
            gsl = slice(g * LANES, (g + 1) * LANES)
            vn = (scr[nseg + t_idx][:, sl] - mu) * rstd * gva_ref[:, gsl] + bva_ref[:, gsl]
            if emit_vn:
                vn_ref[:, gsl] = vn
            vb = vn.astype(BF16)
            u = scr[t_idx][:, sl]
            z = scr[2 * nseg + t_idx][:, sl]
            for c in range(tm // CHUNK):
                rs = slice(c * CHUNK, (c + 1) * CHUNK)
                mixed = jnp.dot(ws_ref[g], vb[rs], preferred_element_type=F32) + bs_ref[g]
                a_ref[rs, gsl] = (u[rs] * mixed * (z[rs] * _sigmoid(z[rs]))).astype(BF16)


def _branch_a(h, w_a, g_va, b_va, ws, bs, *, emit_vn, name):
    m, kdim = h.shape
    d_a = g_va.shape[0]
    tm = _pick(m, (512, 256, 128))
    tn = _pick(d_a, (512, 256, 128))
    ng = d_a // LANES
    out_shape = [jax.ShapeDtypeStruct((m, d_a), BF16)]
    out_specs = [pl.BlockSpec((tm, d_a), lambda i, j: (i, 0))]
    if emit_vn:
        out_shape.append(jax.ShapeDtypeStruct((m, d_a), F32))
        out_specs.append(pl.BlockSpec((tm, d_a), lambda i, j: (i, 0)))
    return pl.pallas_call(
        functools.partial(_branch_a_kernel, tm=tm, tn=tn, d_a=d_a, emit_vn=emit_vn),
        grid=(m // tm, 3 * d_a // tn),
        in_specs=[pl.BlockSpec((tm, kdim), lambda i, j: (i, 0)),
                  pl.BlockSpec((tn, kdim), lambda i, j: (j, 0)),
                  pl.BlockSpec((1, d_a), lambda i, j: (0, 0)),
                  pl.BlockSpec((1, d_a), lambda i, j: (0, 0)),
                  pl.BlockSpec((ng, CHUNK, CHUNK), lambda i, j: (0, 0, 0)),
                  pl.BlockSpec((ng, CHUNK, LANES), lambda i, j: (0, 0, 0))],
        out_specs=out_specs,
        out_shape=out_shape,
        scratch_shapes=[pltpu.VMEM((3 * d_a // tn, tm, tn), F32)],
        compiler_params=_cparams(("parallel", "arbitrary")),
        name=name,
    )(h, w_a, g_va.reshape(1, d_a), b_va.reshape(1, d_a), ws, bs)


def _sortable_key(score):
    bits = lax.bitcast_convert_type(score, I32)
    return jnp.where(bits < 0, (bits ^ 0x7FFFFFFF) + 1, bits)


def _kth_largest_key(count_ge, shape, topk):
    n_valid = count_ge(jnp.full(shape, INT_MIN + 1, I32))

    def unresolved(cnt):
        return (jnp.max(jnp.where(jnp.logical_and(cnt != topk, n_valid >= topk), 1.0, 0.0)) > 0.0).astype(I32)

    def cond(state):
        b, _, _, go = state
        return jnp.logical_and(b < 32, go > 0)

    def body(state):
        b, lo, cnt_lo, _ = state
        cand = lo + lax.shift_left(jnp.int32(1), jnp.int32(31) - b)
        cnt = count_ge(cand)
        take = cnt >= topk
        lo = jnp.where(take, cand, lo)
        cnt_lo = jnp.where(take, cnt, cnt_lo)
        return b + 1, lo, cnt_lo, unresolved(cnt_lo)

    lo0 = jnp.full(shape, INT_MIN, I32)
    cnt0 = jnp.full(shape, 2.0 ** 31, F32)
    _, thr, n_ge, _ = lax.while_loop(cond, body, (jnp.int32(0), lo0, cnt0, jnp.int32(1)))
    return thr, n_ge


def _tie_index_cut(count_eq_below, need, shape, nbits):
    def body(b, ans):
        cand = ans + lax.shift_left(jnp.int32(1), jnp.int32(nbits - 1) - b)
        return jnp.where(count_eq_below(cand) < need, cand, ans)
    return lax.fori_loop(0, nbits, body, jnp.zeros(shape, I32))


def _idx_kernel(qi_ref, wi_ref, ki_ref, mask_ref, key_scr, wb_scr, *, tq, kc, n_idx, n_chunks_total, topk, nbits):
    i = pl.program_id(0)
    nch = i + 1
    per = kc // LANES
    for h in range(n_idx):
        wb_scr[h] = jnp.broadcast_to(wi_ref[:, h:h + 1], (tq, LANES))
    q_pos = i * tq + lax.broadcasted_iota(I32, (1, tq), 1)
    sub = lax.broadcasted_iota(I32, (kc, 1), 0)

    def score_chunk(c, carry):
        kic = ki_ref[pl.ds(pl.multiple_of(c * kc, kc), kc), :]
        acc = [jnp.zeros((tq, LANES), F32) for _ in range(per)]
        for h in range(n_idx):
            d = lax.dot_general(qi_ref[:, h * LANES:(h + 1) * LANES], kic, _NT, preferred_element_type=F32)
            w = wb_scr[h]
            for j in range(per):
                acc[j] = acc[j] + jnp.maximum(d[:, j * LANES:(j + 1) * LANES], 0.0) * w
        score_t = jnp.concatenate(acc, axis=1).T
        key_scr[c] = jnp.where(c * kc + sub <= q_pos, _sortable_key(score_t), INT_MIN)
        return carry

    lax.fori_loop(0, nch, score_chunk, 0)

    def count(pred):
        def body(c, cnt):
            hit = jnp.where(pred(key_scr[c], c * kc + sub), 1.0, 0.0)
            return cnt + jnp.sum(hit.reshape(kc // 8, 8, tq), axis=0)
        cnt = lax.fori_loop(0, nch, body, jnp.zeros((8, tq), F32))
        return jnp.sum(cnt, axis=0, keepdims=True)

    def write_mask(pred):
        def body(c, carry):
            mask_ref[0, c] = jnp.where(pred(key_scr[c], c * kc + sub), 1.0, 0.0).astype(BF16)
            return carry
        lax.fori_loop(0, nch, body, 0)

    thr, n_ge = _kth_largest_key(lambda cand: count(lambda k, kidx: k >= cand), (1, tq), topk)
    tie = jnp.logical_and(n_ge > topk, thr > INT_MIN)
    thr_valid = jnp.maximum(thr, INT_MIN + 1)
    has_tie = jnp.max(jnp.where(tie, 1.0, 0.0)) > 0.0

    def write_fast():
        write_mask(lambda k, kidx: k >= thr_valid)

    def write_ties():
        n_gt = count(lambda k, kidx: k > thr)
        need = topk - n_gt
        cut = _tie_index_cut(lambda cand: count(lambda k, kidx: jnp.logical_and(k == thr, kidx < cand)),
                             need, (1, tq), nbits)
        cut = jnp.where(tie, cut, jnp.int32(2 ** 30))
        write_mask(lambda k, kidx: jnp.logical_and(k >= thr_valid, jnp.logical_or(k > thr, kidx <= cut)))

    lax.cond(has_tie, write_ties, write_fast)

    def fill(c, carry):
        mask_ref[0, c] = jnp.zeros((kc, tq), BF16)
        return carry
    lax.fori_loop(nch, n_chunks_total, fill, 0)


def _prompt_select(qi, wi, ki, *, n_idx, name):
    s = qi.shape[0]
    tq = kc = _pick(s, (256, 128))
    nq = s // tq
    topk = min(TOPK_MAX, s // 4)
    nbits = max(1, int(np.ceil(np.log2(s))))
    return pl.pallas_call(
        functools.partial(_idx_kernel, tq=tq, kc=kc, n_idx=n_idx, n_chunks_total=nq, topk=topk, nbits=nbits),
        grid=(nq,),
        in_specs=[pl.BlockSpec((tq, n_idx * LANES), lambda i: (i, 0)),
                  pl.BlockSpec((tq, LANES), lambda i: (i, 0)),
                  pl.BlockSpec((s, LANES), lambda i: (0, 0))],
        out_specs=pl.BlockSpec((1, nq, kc, tq), lambda i: (i, 0, 0, 0)),
        out_shape=jax.ShapeDtypeStruct((nq, nq, kc, tq), BF16),
        scratch_shapes=[pltpu.VMEM((nq, kc, tq), I32), pltpu.VMEM((n_idx, tq, LANES), F32)],
        compiler_params=_cparams(("parallel",)),
        name=name,
    )(qi, wi, ki)


def _attn_kernel(shift_ref, q_ref, k_ref, vt_ref, mask_ref, o_ref, m_scr, acc_scr, *, rq, rk, kc, hps, scale2,
                 fixed_shift):
    i = pl.program_id(1)
    acc_scr[...] = jnp.zeros(acc_scr.shape, F32)
    if not fixed_shift:
        m_scr[...] = jnp.full(m_scr.shape, NEG, F32)
    ones = jnp.ones((acc_scr.shape[1] - LANES, kc), BF16)
    shift = shift_ref[0]

    def body(c, carry):
        keep = jnp.concatenate(
            [jnp.concatenate([mask_ref[a, c * rk + b] for a in range(rq)], axis=1) for b in range(rk)], axis=0)
        rows = pl.ds(pl.multiple_of(c * kc, kc), kc)
        heads = [slice(hh * LANES, (hh + 1) * LANES) for hh in range(hps)]
        logits = [lax.dot_general(k_ref[rows, sl], q_ref[:, sl], _NT, preferred_element_type=F32) for sl in heads]
        for hh, sl in enumerate(heads):
            v_ext = jnp.concatenate([vt_ref[c, sl, :], ones], axis=0)
            if fixed_shift:
                p = jnp.exp2(logits[hh] * scale2 - shift).astype(BF16) * keep
                acc_scr[hh] = acc_scr[hh] + jnp.dot(v_ext, p, preferred_element_type=F32)
            else:
                s = logits[hh] * scale2 + (keep.astype(F32) - 1.0) * (-NEG)
                m = m_scr[hh]
                m_new = jnp.maximum(m, jnp.max(s, axis=0, keepdims=True))
                alpha = jnp.exp2(m - m_new)
                p = jnp.exp2(s - m_new).astype(BF16)
                acc_scr[hh] = alpha * acc_scr[hh] + jnp.dot(v_ext, p, preferred_element_type=F32)
                m_scr[hh] = m_new
        return carry

    lax.fori_loop(0, i + 1, body, 0)
    for hh in range(hps):
        acc = acc_scr[hh]
        o_ref[:, hh * LANES:(hh + 1) * LANES] = (acc[:LANES] / acc[LANES:LANES + 1]).T.astype(o_ref.dtype)


_FIXED_SHIFT_MAX = 50.0


def _prompt_attention(q, k, vt, mask, g_q, g_k, *, name):
    s, d_b = q.shape
    nqs, nks, kcs, tqs = mask.shape
    nkc, _, kc = vt.shape
    tq = kc
    rq, rk = tq // tqs, kc // kcs
    hps = _pick(d_b // LANES, (4, 2, 1))
    wblk = hps * LANES
    once = pl.Buffered(1)
    scale2 = float(LANES ** -0.5 * np.log2(np.e))
    bound = (1.01 * LANES * scale2) * jnp.max(jnp.abs(g_q)) * jnp.max(jnp.abs(g_k))

    def run(fixed_shift):
        return pl.pallas_call(
            functools.partial(_attn_kernel, rq=rq, rk=rk, kc=kc, hps=hps, scale2=scale2, fixed_shift=fixed_shift),
            grid=(d_b // wblk, s // tq),
            in_specs=[pl.BlockSpec(memory_space=pltpu.SMEM),
                      pl.BlockSpec((tq, wblk), lambda h, i: (i, h)),
                      pl.BlockSpec((s, wblk), lambda h, i: (0, h), pipeline_mode=once),
                      pl.BlockSpec((nkc, wblk, kc), lambda h, i: (0, h, 0), pipeline_mode=once),
                      pl.BlockSpec((rq, nks, kcs, tqs), lambda h, i: (i, 0, 0, 0))],
            out_specs=pl.BlockSpec((tq, wblk), lambda h, i: (i, h)),
            out_shape=jax.ShapeDtypeStruct((s, d_b), BF16),
            scratch_shapes=[pltpu.VMEM((hps, 1, tq), F32), pltpu.VMEM((hps, LANES + 16, tq), F32)],
            compiler_params=_cparams(("parallel", "arbitrary")),
            name=name + ("_fixed_shift" if fixed_shift else "_online"),
        )(bound.reshape(1).astype(F32), q, k, vt, mask)

    return lax.cond(bound <= _FIXED_SHIFT_MAX, lambda: run(True), lambda: run(False))


def _sample_idx_kernel(pt_ref, *refs, g_pages, n_steps, n_idx, dec_seq, topk, nbits, past, att_keys):
    del pt_ref
    pages = refs[:g_pages]
    qi_ref, wcol_ref, kinew_ref, bias_past_ref, bias_new_ref, kb_scr, key_scr, keyn_scr = refs[g_pages:]
    j = pl.program_id(1)
    pg = g_pages * PAGE_SIZE
    rows = 8
    row_id = lax.broadcasted_iota(I32, (rows, 1), 0)

    def scores_of(kmat):
        d = lax.dot_general(qi_ref[0], kmat, _NT, preferred_element_type=F32)
        wr = jnp.maximum(d, 0.0) * wcol_ref[0]
        s4 = jnp.sum(wr.reshape(dec_seq, n_idx, kmat.shape[0]), axis=1)
        return jnp.concatenate([s4, jnp.zeros((rows - dec_seq, kmat.shape[0]), F32)], axis=0)

    for r in range(g_pages):
        kb_scr[r * PAGE_SIZE:(r + 1) * PAGE_SIZE, :] = pages[r][0].astype(BF16)
    key_scr[j] = jnp.where(row_id < dec_seq, _sortable_key(scores_of(kb_scr[...])), INT_MIN)

    @pl.when(j == n_steps - 1)
    def _():
        sn = scores_of(kinew_ref[0])
        coln = lax.broadcasted_iota(I32, (1, LANES), 1)
        okn = jnp.logical_and(jnp.logical_and(coln < dec_seq, coln <= row_id), row_id < dec_seq)
        keyn_scr[...] = jnp.where(okn, _sortable_key(sn), INT_MIN)

        def count(pred):
            cnt = jnp.zeros((rows, LANES), F32)
            for c in range(n_steps):
                p = pred(key_scr[c], c * pg + lax.broadcasted_iota(I32, (1, pg), 1))
                for jj in range(pg // LANES):
                    cnt = cnt + jnp.where(p[:, jj * LANES:(jj + 1) * LANES], 1.0, 0.0)
            cnt = cnt + jnp.where(pred(keyn_scr[...], past + coln), 1.0, 0.0)
            return jnp.sum(cnt, axis=-1, keepdims=True)

        thr, n_ge = _kth_largest_key(lambda cand: count(lambda k, col: k >= cand), (rows, 1), topk)
        n_gt = count(lambda k, col: k > thr)
        tie = jnp.logical_and(n_ge > topk, thr > INT_MIN)
        thr_valid = jnp.maximum(thr, INT_MIN + 1)
        need = topk - n_gt
        cut = _tie_index_cut(
            lambda cand: count(lambda k, col: jnp.logical_and(k == thr, col < cand)), need, (rows, 1), nbits)
        cut = jnp.where(tie, cut, jnp.int32(2 ** 30))

        def sel(k, col):
            return jnp.logical_and(k >= thr_valid, jnp.logical_or(k > thr, col <= cut))

        n_sub = pg // att_keys
        for c in range(n_steps):
            b = jnp.where(sel(key_scr[c], c * pg + lax.broadcasted_iota(I32, (1, pg), 1)), 0.0, NEG)
            for u in range(n_sub):
                bias_past_ref[0, n_sub * c + u] = b[:, u * att_keys:(u + 1) * att_keys]
        bias_new_ref[0] = jnp.where(sel(keyn_scr[...], past + coln), 0.0, NEG)


def _sample_select(page_table, cache_kidx, qi_rows, wcol, kinew, *, n_idx, dec_seq, name):
    bd, n_pages = page_table.shape
    g_pages = _pick(n_pages, (32, 16, 8, 4, 2, 1))
    n_steps = n_pages // g_pages
    pg = g_pages * PAGE_SIZE
    att_keys = _pick(pg, (8 * PAGE_SIZE, 4 * PAGE_SIZE, 2 * PAGE_SIZE, PAGE_SIZE))
    n_att = n_steps * (pg // att_keys)
    past = n_pages * PAGE_SIZE
    topk = min(TOPK_MAX, (past + dec_seq) // 4)
    nbits = max(1, int(np.ceil(np.log2(past + LANES))))
    rq = dec_seq * n_idx

    def page_spec(r):
        return pl.BlockSpec((1, PAGE_SIZE, LANES), lambda b, j, pt: (pt[b, j * g_pages + r], 0, 0))

    grid_spec = pltpu.PrefetchScalarGridSpec(
        num_scalar_prefetch=1,
        grid=(bd, n_steps),
        in_specs=[page_spec(r) for r in range(g_pages)] + [
            pl.BlockSpec((1, rq, LANES), lambda b, j, pt: (b, 0, 0)),
            pl.BlockSpec((1, rq, 1), lambda b, j, pt: (b, 0, 0)),
            pl.BlockSpec((1, LANES, LANES), lambda b, j, pt: (b, 0, 0))],
        out_specs=[pl.BlockSpec((1, n_att, 8, att_keys), lambda b, j, pt: (b, 0, 0, 0)),
                   pl.BlockSpec((1, 8, LANES), lambda b, j, pt: (b, 0, 0))],
        scratch_shapes=[pltpu.VMEM((pg, LANES), BF16), pltpu.VMEM((n_steps, 8, pg), I32),
                        pltpu.VMEM((8, LANES), I32)],
    )
    return pl.pallas_call(
        functools.partial(_sample_idx_kernel, g_pages=g_pages, n_steps=n_steps, n_idx=n_idx,
                          dec_seq=dec_seq, topk=topk, nbits=nbits, past=past, att_keys=att_keys),
        grid_spec=grid_spec,
        out_shape=[jax.ShapeDtypeStruct((bd, n_att, 8, att_keys), F32),
                   jax.ShapeDtypeStruct((bd, 8, LANES), F32)],
        compiler_params=_cparams(("parallel", "arbitrary")),
        name=name,
    )(page_table, *([cache_kidx] * g_pages), qi_rows, wcol, kinew)


def _sample_attn_kernel(pt_ref, *refs, g_pages, n_steps, n_heads, scale):
    del pt_ref
    kpages = refs[:g_pages]
    vpages = refs[g_pages:2 * g_pages]
    (qblk_ref, bias_ref, biasn_ref, knew_ref, vnew_ref, o_ref,
     kb_scr, vb_scr, m_scr, l_scr, acc_scr) = refs[2 * g_pages:]
    j = pl.program_id(1)
    rows = 8 * n_heads

    @pl.when(j == 0)
    def _():
        m_scr[...] = jnp.full(m_scr.shape, NEG, F32)
        l_scr[...] = jnp.zeros(l_scr.shape, F32)
        acc_scr[...] = jnp.zeros(acc_scr.shape, F32)

    def logits_t(kb):
        return jnp.dot(kb, qblk_ref[0], preferred_element_type=F32)

    def update(st, vb, bias8):
        s = st.T * scale + jnp.concatenate([bias8] * n_heads, axis=0)
        m = m_scr[...]
        m_new = jnp.maximum(m, jnp.max(s, axis=-1, keepdims=True))
        alpha = jnp.exp2(m - m_new)
        p = jnp.exp2(s - m_new)
        l_scr[...] = alpha * l_scr[...] + jnp.sum(p, axis=-1, keepdims=True)
        acc_scr[...] = alpha * acc_scr[...] + jnp.dot(p.astype(BF16), vb, preferred_element_type=F32)
        m_scr[...] = m_new

    def gather_heads(pages, dst):
        kg = 16
        for r in range(g_pages):
            for g in range(PAGE_SIZE // kg):
                blk = pages[r][0, 0, g * kg * n_heads:(g + 1) * kg * n_heads, :].astype(BF16)
                by_head = pltpu.einshape("khd->hkd", blk.reshape(kg, n_heads, LANES))
                ks = slice(r * PAGE_SIZE + g * kg, r * PAGE_SIZE + (g + 1) * kg)
                for h in range(n_heads):
                    dst[ks, h * LANES:(h + 1) * LANES] = by_head[h]

    gather_heads(kpages, kb_scr)
    st = logits_t(kb_scr[...])
    gather_heads(vpages, vb_scr)
    update(st, vb_scr[...], bias_ref[0, 0])

    @pl.when(j == n_steps - 1)
    def _():
        update(logits_t(knew_ref[0]), vnew_ref[0], biasn_ref[0])
        for h in range(n_heads):
            rs = slice(h * 8, (h + 1) * 8)
            cs = slice(h * LANES, (h + 1) * LANES)
            o_ref[0, :, cs] = acc_scr[rs, cs] / l_scr[rs, :]


def _sample_attention(page_table, cache_k, cache_v, layer, qblk, bias_past, bias_new, knew, vnew, *, n_heads, name):
    bd, n_pages = page_table.shape
    d_b = n_heads * LANES
    n_steps = bias_past.shape[1]
    g_pages = n_pages // n_steps
    pg = g_pages * PAGE_SIZE
    rows = 8 * n_heads

    def page_spec(r):
        return pl.BlockSpec((1, 1, PAGE_SIZE * n_heads, LANES),
                            lambda b, j, pt: (layer, pt[b, j * g_pages + r], 0, 0))

    grid_spec = pltpu.PrefetchScalarGridSpec(
        num_scalar_prefetch=1,
        grid=(bd, n_steps),
        in_specs=[page_spec(r) for r in range(g_pages)] * 2 + [
            pl.BlockSpec((1, d_b, rows), lambda b, j, pt: (b, 0, 0)),
            pl.BlockSpec((1, 1, 8, pg), lambda b, j, pt: (b, j, 0, 0)),
            pl.BlockSpec((1, 8, LANES), lambda b, j, pt: (b, 0, 0)),
            pl.BlockSpec((1, LANES, d_b), lambda b, j, pt: (b, 0, 0)),
            pl.BlockSpec((1, LANES, d_b), lambda b, j, pt: (b, 0, 0))],
        out_specs=pl.BlockSpec((1, 8, d_b), lambda b, j, pt: (b, 0, 0)),
        scratch_shapes=[pltpu.VMEM((pg, d_b), BF16), pltpu.VMEM((pg, d_b), BF16),
                        pltpu.VMEM((rows, 1), F32), pltpu.VMEM((rows, 1), F32), pltpu.VMEM((rows, d_b), F32)],
    )
    return pl.pallas_call(
        functools.partial(_sample_attn_kernel, g_pages=g_pages, n_steps=n_steps, n_heads=n_heads,
                          scale=float(LANES ** -0.5 * np.log2(np.e))),
        grid_spec=grid_spec,
        out_shape=jax.ShapeDtypeStruct((bd, 8, d_b), F32),
        compiler_params=_cparams(("parallel", "arbitrary")),
        name=name,
    )(page_table, *([cache_k] * g_pages), *([cache_v] * g_pages), qblk, bias_past, bias_new, knew, vnew)


def _merge_kernel(h_ref, a_ref, attn_ref, zs_ref, wga_ref, wgb_ref, wua_ref, wub_ref, o_ref, bg_scr):
    @pl.when(pl.program_id(1) == 0)
    def _():
        bg_scr[...] = (attn_ref[...].astype(F32) * zs_ref[...].astype(F32)).astype(BF16)

    h = h_ref[...]
    ga = lax.dot_general(h, wga_ref[...], _NT, preferred_element_type=F32)
    gb = lax.dot_general(h, wgb_ref[...], _NT, preferred_element_type=F32)
    a_up = jnp.dot(a_ref[...], wua_ref[...].astype(BF16), preferred_element_type=F32)
    b_up = jnp.dot(bg_scr[...], wub_ref[...].astype(BF16), preferred_element_type=F32)
    o_ref[...] = (_sigmoid(ga) * a_up + _sigmoid(gb) * b_up).astype(o_ref.dtype)


def _merge(h, a_gated, attn, zs, w_ga, w_gb, w_ua, w_ub, *, name):
    m, d = h.shape
    d_a, d_b = a_gated.shape[1], attn.shape[1]
    tm = _pick(m, (512, 256, 128))
    tn = _pick(d, (256, 128))
    row = lambda i, j: (i, 0)
    col = lambda i, j: (0, j)
    return pl.pallas_call(
        _merge_kernel,
        grid=(m // tm, d // tn),
        in_specs=[pl.BlockSpec((tm, d), row), pl.BlockSpec((tm, d_a), row),
                  pl.BlockSpec((tm, d_b), row), pl.BlockSpec((tm, d_b), row),
                  pl.BlockSpec((tn, d), lambda i, j: (j, 0)), pl.BlockSpec((tn, d), lambda i, j: (j, 0)),
                  pl.BlockSpec((d_a, tn), col), pl.BlockSpec((d_b, tn), col)],
        out_specs=pl.BlockSpec((tm, tn), lambda i, j: (i, j)),
        out_shape=jax.ShapeDtypeStruct((m, d), BF16),
        scratch_shapes=[pltpu.VMEM((tm, d_b), BF16)],
        compiler_params=_cparams(("parallel", "arbitrary")),
        name=name,
    )(h, a_gated, attn, zs, w_ga, w_gb, w_ua, w_ub)


def _resid_kernel(mg_ref, w_ref, x_ref, o_ref):
    o_ref[...] = x_ref[...] + jnp.dot(mg_ref[...], w_ref[...].astype(BF16), preferred_element_type=F32)


def _resid_proj(merged, w_o, x, *, name):
    m, d = x.shape
    tm = _pick(m, (1024, 512, 256, 128))
    tn = _pick(d, (512, 256, 128))
    return pl.pallas_call(
        _resid_kernel,
        grid=(m // tm, d // tn),
        in_specs=[pl.BlockSpec((tm, merged.shape[1]), lambda i, j: (i, 0)),
                  pl.BlockSpec((merged.shape[1], tn), lambda i, j: (0, j)),
                  pl.BlockSpec((tm, tn), lambda i, j: (i, j))],
        out_specs=pl.BlockSpec((tm, tn), lambda i, j: (i, j)),
        out_shape=jax.ShapeDtypeStruct((m, d), F32),
        compiler_params=_cparams(("parallel", "arbitrary")),
        name=name,
    )(merged, w_o, x)


def _ple_kernel(hn_ref, wg_ref, p_ref, wp_ref, x1_ref, o_ref):
    gate = _sigmoid(jnp.dot(hn_ref[...], wg_ref[...].astype(BF16), preferred_element_type=F32))
    emb = jnp.dot(p_ref[...], wp_ref[...].astype(BF16), preferred_element_type=F32)
    o_ref[...] = x1_ref[...] + gate * emb


def _ple(hn, w_gate, p, w_proj, x1, *, name):
    m, d = x1.shape
    tm = _pick(m, (1024, 512, 256, 128))
    tn = _pick(d, (512, 256, 128))
    return pl.pallas_call(
        _ple_kernel,
        grid=(m // tm, d // tn),
        in_specs=[pl.BlockSpec((tm, d), lambda i, j: (i, 0)),
                  pl.BlockSpec((d, tn), lambda i, j: (0, j)),
                  pl.BlockSpec((tm, p.shape[1]), lambda i, j: (i, 0)),
                  pl.BlockSpec((p.shape[1], tn), lambda i, j: (0, j)),
                  pl.BlockSpec((tm, tn), lambda i, j: (i, j))],
        out_specs=pl.BlockSpec((tm, tn), lambda i, j: (i, j)),
        out_shape=jax.ShapeDtypeStruct((m, d), F32),
        compiler_params=_cparams(("parallel", "arbitrary")),
        name=name,
    )(hn, w_gate, p, w_proj, x1)


def _rope_tables(pos):
    half = LANES // 2
    inv = ROPE_THETA ** (-jnp.arange(half, dtype=F32) / half)
    ang = pos.astype(F32)[:, None] * inv[None, :]
    cos, sin = jnp.cos(ang), jnp.sin(ang)
    return jnp.concatenate([cos, cos], axis=-1), jnp.concatenate([-sin, sin], axis=-1)


def _input_weights(w_in, d_model, d_a, d_b, n_idx):
    widths = [3 * d_a, d_b, d_b, d_b, d_b, n_idx * LANES, LANES + n_idx, d_model, d_model]
    offs = [int(v) for v in np.concatenate([[0], np.cumsum(widths)])]
    w_t = jnp.swapaxes(w_in, 0, 1).astype(BF16)
    cols = dict(zip(("a", "q", "k", "v", "zb", "qi", "kiwi"), offs[:7]))
    return w_t, cols, w_t[offs[7]:offs[8]], w_t[offs[8]:offs[9]]


def _layer(x, p, pos, tag, w, ws, bs, *, n_heads, n_idx, emit_vn, v_chunk=0):
    m, d_model = x.shape
    cosf, sinf = _rope_tables(pos)
    h = _rmsnorm(x, w["g_norm"], f"{tag}_norm")
    w_all, cols, d_b = w["w_all"], w["cols"], n_heads * LANES
    a_out = _branch_a(h, w_all, w["g_va"], w["b_va"], ws, bs, emit_vn=emit_vn, name=f"{tag}_branch_a")
    gq = jnp.tile(w["g_q"], n_heads).reshape(1, -1)
    gk = jnp.tile(w["g_k"], n_heads).reshape(1, -1)
    (q_bf,) = _proj(h, w_all, col_off=cols["q"], n=d_b, kind="norm_rope", n_f32=0, n_bf16=1, gain=gq, cos=cosf,
                    sin=sinf, name=f"{tag}_q")
    k_f32, k_bf = _proj(h, w_all, col_off=cols["k"], n=d_b, kind="norm_rope", n_f32=1, n_bf16=1, gain=gk, cos=cosf,
                        sin=sinf, name=f"{tag}_k")
    if v_chunk:
        v_f32, v_bf = _proj(h, w_all, col_off=cols["v"], n=d_b, kind="plain", n_f32=1, n_bf16=0, t_chunk=v_chunk,
                            name=f"{tag}_v")
    else:
        v_f32, v_bf = _proj(h, w_all, col_off=cols["v"], n=d_b, kind="plain", n_f32=1, n_bf16=1, name=f"{tag}_v")
    (zs,) = _proj(h, w_all, col_off=cols["zb"], n=d_b, kind="silu", n_f32=0, n_bf16=1, name=f"{tag}_zb")
    (qi,) = _proj(h, w_all, col_off=cols["qi"], n=n_idx * LANES, kind="rope", n_f32=0, n_bf16=1, cos=cosf, sin=sinf,
                  name=f"{tag}_qi")
    ki_f32, ki_bf, wi = _proj(h, w_all, col_off=cols["kiwi"], n=2 * LANES, kind="kiwi", n_f32=0, n_bf16=0, cos=cosf,
                              sin=sinf, scale=float(n_idx ** -0.5 * LANES ** -0.5), name=f"{tag}_kiwi")
    return dict(h=h, a=a_out, q=q_bf, k=k_f32, kb=k_bf, v=v_f32, vb=v_bf, zs=zs, qi=qi, ki=ki_f32, kib=ki_bf, wi=wi)


def _finish(x, p, t, attn, tag, w):
    merged = _merge(t["h"], t["a"][0], attn, t["zs"], w["w_ga"], w["w_gb"], w["w_ua"], w["w_ub"], name=f"{tag}_merge")
    x1 = _resid_proj(merged, w["w_o"], x, name=f"{tag}_resid")
    hn = _rmsnorm(x1, w["g_ple"], f"{tag}_ple_norm")
    return _ple(hn, w["w_pg"], p.astype(BF16), w["w_pp"], x1, name=f"{tag}_ple")


def kernel(x_prompt, x_sample, p_prompt, p_sample, cache_k, cache_v, cache_kidx, page_table, g_norm, w_in, g_va,
           b_va, w_s, b_s, g_q, g_k, w_up_a, w_up_b, w_o, g_ple, w_ple_gate, w_ple_proj):
    depth = w_in.shape[0]
    batch, seq, d_model = x_prompt.shape
    bd, dec_seq, _ = x_sample.shape
    n_phys, _, n_heads, head_dim = cache_k.shape[1:]
    n_groups = w_s.shape[1]
    d_a, d_b = w_up_a.shape[1], w_up_b.shape[1]
    n_idx = w_in.shape[2] - (3 * d_a + 4 * d_b + LANES + 2 * d_model)
    n_idx = n_idx // (LANES + 1)
    n_pages = page_table.shape[1]
    past = n_pages * PAGE_SIZE
    assert batch == 1 and head_dim == LANES and cache_kidx.shape[-1] == LANES and d_a == n_groups * LANES
    assert seq % CHUNK == 0 and bd * dec_seq == CHUNK and dec_seq <= 8 and d_b == n_heads * LANES
    ms = bd * dec_seq

    yp = x_prompt.reshape(seq, d_model)
    ys = x_sample.reshape(ms, d_model)
    pos_p = jnp.arange(seq)
    pos_s = jnp.tile(past + jnp.arange(dec_seq), bd)
    outs = {k: [] for k in ("kp", "vp", "kip", "ks", "vs", "kis", "vas")}
    tril = jnp.tril(jnp.ones((CHUNK, CHUNK), bool))
    cache_k2 = cache_k.reshape(depth, n_phys, PAGE_SIZE * n_heads, LANES)
    cache_v2 = cache_v.reshape(depth, n_phys, PAGE_SIZE * n_heads, LANES)
    for i in range(depth):
        w_all, cols, w_ga, w_gb = _input_weights(w_in[i], d_model, d_a, d_b, n_idx)
        w = dict(w_all=w_all, cols=cols, w_ga=w_ga, w_gb=w_gb)
        w.update(g_norm=g_norm[i], g_va=g_va[i], b_va=b_va[i], g_q=g_q[i], g_k=g_k[i], g_ple=g_ple[i],
                 w_ua=w_up_a[i], w_ub=w_up_b[i], w_o=w_o[i], w_pg=w_ple_gate[i], w_pp=w_ple_proj[i])
        ws_p = jnp.where(tril[None], w_s[i], 0.0).astype(BF16)
        bs_p = jnp.broadcast_to(b_s[i][:, :, None], (n_groups, CHUNK, LANES))
        small = jnp.where(tril[None, :dec_seq, :dec_seq], w_s[i][:, :dec_seq, :dec_seq], 0.0)
        ws_s = jnp.einsum("ab,gts->gatbs", jnp.eye(bd, dtype=F32), small).reshape(n_groups, ms, ms).astype(BF16)
        bs_s = jnp.broadcast_to(jnp.tile(b_s[i][:, :dec_seq], (1, bd))[:, :, None], (n_groups, ms, LANES))

        tp = _layer(yp, p_prompt[i, 0], pos_p, "prompt", w, ws_p, bs_p, n_heads=n_heads, n_idx=n_idx, emit_vn=False,
                    v_chunk=_pick(seq, (512, 256, 128)))
        mask = _prompt_select(tp["qi"], tp["wi"], tp["kib"], n_idx=n_idx, name="prompt_select")
        attn_p = _prompt_attention(tp["q"], tp["kb"], tp["vb"], mask, g_q[i], g_k[i], name="prompt_attention")
        yp_new = _finish(yp, p_prompt[i, 0], tp, attn_p, "prompt", w)

        tsm = _layer(ys, p_sample[i].reshape(ms, -1), pos_s, "sample", w, ws_s, bs_s, n_heads=n_heads, n_idx=n_idx,
                     emit_vn=True)
        qi_rows = tsm["qi"].reshape(bd, dec_seq * n_idx, LANES)
        wcol = tsm["wi"][:, :n_idx].reshape(bd, dec_seq * n_idx, 1)
        kinew = jnp.pad(tsm["kib"].reshape(bd, dec_seq, LANES), ((0, 0), (0, LANES - dec_seq), (0, 0)))
        bias_past, bias_new = _sample_select(page_table, cache_kidx[i], qi_rows, wcol, kinew, n_idx=n_idx,
                                             dec_seq=dec_seq, name="sample_select")
        q4 = jnp.pad(tsm["q"].reshape(bd, dec_seq, n_heads, LANES), ((0, 0), (0, 8 - dec_seq), (0, 0), (0, 0)))
        qblk = jnp.einsum("bthd,hg->bhdgt", q4, jnp.eye(n_heads, dtype=BF16)).reshape(bd, d_b, n_heads * 8)
        padrows = ((0, 0), (0, LANES - dec_seq), (0, 0))
        knew = jnp.pad(tsm["kb"].reshape(bd, dec_seq, d_b), padrows)
        vnew = jnp.pad(tsm["vb"].reshape(bd, dec_seq, d_b), padrows)
        attn_s8 = _sample_attention(page_table, cache_k2, cache_v2, i, qblk, bias_past, bias_new,
                                    knew, vnew, n_heads=n_heads, name="sample_attention")
        attn_s = attn_s8[:, :dec_seq].reshape(ms, d_b).astype(BF16)
        ys_new = _finish(ys, p_sample[i].reshape(ms, -1), tsm, attn_s, "sample", w)

        outs["kp"].append(tp["k"].reshape(batch, seq, n_heads, LANES))
        outs["vp"].append(tp["v"].reshape(batch, seq, n_heads, LANES))
        outs["kip"].append(tp["ki"].reshape(batch, seq, LANES))
        outs["ks"].append(tsm["k"].reshape(bd, dec_seq, n_heads, LANES))
        outs["vs"].append(tsm["v"].reshape(bd, dec_seq, n_heads, LANES))
        outs["kis"].append(tsm["ki"].reshape(bd, dec_seq, LANES))
        outs["vas"].append(tsm["a"][1].reshape(bd, dec_seq, d_a))
        yp, ys = yp_new, ys_new

    return (yp.reshape(batch, seq, d_model), ys.reshape(bd, dec_seq, d_model),
            jnp.stack(outs["kp"]), jnp.stack(outs["vp"]), jnp.stack(outs["kip"]),
            jnp.stack(outs["ks"]), jnp.stack(outs["vs"]), jnp.stack(outs["kis"]), jnp.stack(outs["vas"]))
```
